```python
import math
import jax, jax.numpy as jnp
from jax import lax
import numpy as np

D_MODEL = 1024
BATCH = 2
SEQ = 8192
DEPTH = 1
DEC_BATCH = 128
DEC_SEQ = 1
PAST_LEN = 8192
PAGE_SIZE = 128

SSM_WIDTH = D_MODEL // 2
SSM_GROUP = 16
SSM_GROUPS = SSM_WIDTH // SSM_GROUP
SSM_STATE = 64
N_HEADS = 8
HEAD_DIM = 64
ATT_WIDTH = N_HEADS * HEAD_DIM
MOBA_BLOCK = 256
MOBA_TOPK = 3
Q_CHUNK = 64
D_FF = 2816
CONV_W = 3
NORM_EPS = 1e-6
IN_WIDTH = SSM_WIDTH + 3 * ATT_WIDTH + 2 * D_MODEL

kernel_name = "hybrid_s5_moba_gated_decoder_step"


def rmsnorm(x, w):
    xf = x.astype(jnp.float32)
    y = xf * lax.rsqrt(jnp.mean(xf * xf, axis=-1, keepdims=True) + NORM_EPS)
    return (y * w.astype(jnp.float32)).astype(x.dtype)


def alibi_slopes():
    return 2.0 ** (-8.0 * jnp.arange(1, N_HEADS + 1, dtype=jnp.float32) / N_HEADS)


def mixer_inputs(x, norm1_w, w_in, q_norm_w, k_norm_w):
    n, t, _ = x.shape
    z = rmsnorm(x, norm1_w) @ w_in
    o1 = SSM_WIDTH
    o2 = o1 + ATT_WIDTH
    o3 = o2 + ATT_WIDTH
    o4 = o3 + ATT_WIDTH
    o5 = o4 + D_MODEL
    u = z[..., :o1]
    q = rmsnorm(z[..., o1:o2].reshape(n, t, N_HEADS, HEAD_DIM), q_norm_w)
    k = rmsnorm(z[..., o2:o3].reshape(n, t, N_HEADS, HEAD_DIM), k_norm_w)
    v = z[..., o3:o4].reshape(n, t, N_HEADS, HEAD_DIM)
    return u, q, k, v, z[..., o4:o5], z[..., o5:]


def _complex_affine_combine(e1, e2):
    a1r, a1i, b1r, b1i = e1
    a2r, a2i, b2r, b2i = e2
    return (a2r * a1r - a2i * a1i,
            a2r * a1i + a2i * a1r,
            a2r * b1r - a2i * b1i + b2r,
            a2r * b1i + a2i * b1r + b2i)


def s5_branch(u, h0_re, h0_im, lam_re, lam_im, log_dt, b_re, b_im, c_re, c_im, d_skip, w_glu, b_glu):
    f32 = jnp.float32
    n, t, _ = u.shape
    uf = u.astype(f32)
    ug = uf.reshape(n, t, SSM_GROUPS, SSM_GROUP)
    lr, li = lam_re.astype(f32), lam_im.astype(f32)
    dt = jnp.exp(log_dt.astype(f32))[:, None]
    mag = jnp.exp(lr * dt)
    ar, ai = mag * jnp.cos(li * dt), mag * jnp.sin(li * dt)
    den = lr * lr + li * li
    fr = ((ar - 1.0) * lr + ai * li) / den
    fi = (ai * lr - (ar - 1.0) * li) / den
    br_, bi_ = b_re.astype(f32), b_im.astype(f32)
    bbr = fr[..., None] * br_ - fi[..., None] * bi_
    bbi = fr[..., None] * bi_ + fi[..., None] * br_
    xr = jnp.einsum('ntgc,gsc->ntgs', ug, bbr)
    xi = jnp.einsum('ntgc,gsc->ntgs', ug, bbi)
    h0r, h0i = h0_re.astype(f32), h0_im.astype(f32)
    xr = xr.at[:, 0].add(ar * h0r - ai * h0i)
    xi = xi.at[:, 0].add(ar * h0i + ai * h0r)
    a_r = jnp.broadcast_to(ar, xr.shape)
    a_i = jnp.broadcast_to(ai, xi.shape)
    _, _, hr, hi = lax.associative_scan(_complex_affine_combine, (a_r, a_i, xr, xi), axis=1)
    y = (jnp.einsum('gcs,ntgs->ntgc', c_re.astype(f32), hr)
         - jnp.einsum('gcs,ntgs->ntgc', c_im.astype(f32), hi))
    y = y.reshape(n, t, SSM_WIDTH) + d_skip.astype(f32) * uf
    y = jax.nn.gelu(y)
    y = y * jax.nn.sigmoid(y @ w_glu.astype(f32) + b_glu.astype(f32))
    return y.astype(u.dtype), hr[:, -1], hi[:, -1]


def moba_core(q, q_pos, blk_mean, fetch, slopes):
    f32 = jnp.float32
    nb = blk_mean.shape[1]
    tq = q_pos.shape[0]
    qf = q.astype(f32)
    gate = jnp.einsum('nthd,nbhd->nhtb', qf, blk_mean.astype(f32))
    q_blk = q_pos // MOBA_BLOCK
    past = jnp.arange(nb)[None, :] < q_blk[:, None]
    gate = jnp.where(past[None, None], gate, -jnp.inf)
    _, idx = lax.top_k(gate, MOBA_TOPK)
    own = jnp.broadcast_to(q_blk[None, None, :, None].astype(idx.dtype), idx.shape[:3] + (1,))
    blocks = jnp.concatenate([idx, own], axis=-1)
    slot_ok = jnp.concatenate([jnp.arange(MOBA_TOPK)[None, :] < q_blk[:, None],
                               jnp.ones((tq, 1), dtype=bool)], axis=-1)
    s_pos = blocks[..., None] * MOBA_BLOCK + jnp.arange(MOBA_BLOCK)
    ok = slot_ok[None, None, :, :, None] & (s_pos <= q_pos[None, None, :, None, None])
    k_sel, v_sel = fetch(s_pos)
    sc = jnp.einsum('nthd,nhtjrd->nhtjr', qf, k_sel.astype(f32)) * (HEAD_DIM ** -0.5)
    dist = (q_pos[None, None, :, None, None] - s_pos).astype(f32)
    sc = sc - slopes[None, :, None, None, None] * dist
    sc = jnp.where(ok, sc, -jnp.inf)
    n, h = sc.shape[:2]
    p = jax.nn.softmax(sc.reshape(n, h, tq, -1), axis=-1)
    out = jnp.einsum('nhtc,nhtcd->nthd', p, v_sel.reshape(n, h, tq, -1, HEAD_DIM).astype(f32))
    return out.astype(q.dtype)


def moba_prompt(q, k, v, slopes):
    n, t = q.shape[:2]
    nb = max(-(-t // MOBA_BLOCK), MOBA_TOPK)
    kp = jnp.pad(k.astype(jnp.float32), ((0, 0), (0, nb * MOBA_BLOCK - t), (0, 0), (0, 0)))
    blk_mean = kp.reshape(n, nb, MOBA_BLOCK, N_HEADS, HEAD_DIM).sum(2) / MOBA_BLOCK
    n_idx = jnp.arange(n)[:, None, None, None, None]
    h_idx = jnp.arange(N_HEADS)[None, :, None, None, None]

    def fetch(s_pos):
        s = jnp.clip(s_pos, 0, t - 1)
        return k[n_idx, s, h_idx], v[n_idx, s, h_idx]

    def chunk(c):
        q_c = lax.dynamic_slice_in_dim(q, c * Q_CHUNK, Q_CHUNK, axis=1)
        pos = c * Q_CHUNK + jnp.arange(Q_CHUNK)
        return moba_core(q_c, pos, blk_mean, fetch, slopes)

    out = lax.map(chunk, jnp.arange(t // Q_CHUNK))
    return out.transpose(1, 0, 2, 3, 4).reshape(n, t, ATT_WIDTH)


def moba_sample(q, k_new, v_new, k_pool, v_pool, page_table, slopes):
    n, t_new = q.shape[:2]
    n_pages = PAST_LEN // PAGE_SIZE
    nb = max(-(-(PAST_LEN + t_new) // MOBA_BLOCK), MOBA_TOPK)
    page_sum = jnp.sum(k_pool, axis=1, dtype=jnp.float32)[page_table]
    rows = jnp.concatenate([page_sum, k_new.astype(jnp.float32)], axis=1)
    seg = jnp.concatenate([(jnp.arange(n_pages) * PAGE_SIZE) // MOBA_BLOCK,
                           (PAST_LEN + jnp.arange(t_new)) // MOBA_BLOCK])
    blk_sum = jax.ops.segment_sum(rows.transpose(1, 0, 2, 3), seg, num_segments=nb)
    blk_mean = blk_sum.transpose(1, 0, 2, 3) / MOBA_BLOCK
    n_idx = jnp.arange(n)[:, None, None, None, None]
    h_idx = jnp.arange(N_HEADS)[None, :, None, None, None]

    def fetch(s_pos):
        in_past = (s_pos < PAST_LEN)[..., None]
        page = jnp.clip(s_pos // PAGE_SIZE, 0, n_pages - 1)
        off = s_pos % PAGE_SIZE
        phys = page_table[n_idx, page]
        j = jnp.clip(s_pos - PAST_LEN, 0, t_new - 1)
        k_sel = jnp.where(in_past, k_pool[phys, off, h_idx], k_new[n_idx, j, h_idx])
        v_sel = jnp.where(in_past, v_pool[phys, off, h_idx], v_new[n_idx, j, h_idx])
        return k_sel, v_sel

    pos = PAST_LEN + jnp.arange(t_new)
    return moba_core(q, pos, blk_mean, fetch, slopes).reshape(n, t_new, ATT_WIDTH)


def gated_merge(x, y_ssm, y_att, g_ssm, g_att, w_a, w_b, w_o):
    mix = jax.nn.sigmoid(g_ssm) * (y_ssm @ w_a) + jax.nn.sigmoid(g_att) * (y_att @ w_b)
    return x + mix @ w_o


def conv_ffn(x, buf, norm2_w, w_up, w_gate, conv_w, conv_b, w_down):
    t = x.shape[1]
    h = rmsnorm(x, norm2_w)
    up = h @ w_up
    g = h @ w_gate
    g_ext = jnp.concatenate([buf.astype(g.dtype), g], axis=1)
    conv = conv_b
    for j in range(CONV_W):
        conv = conv + conv_w[j] * g_ext[:, j:j + t]
    y = (jax.nn.gelu(conv) * up) @ w_down
    return x + y, g_ext[:, t:]


def setup_inputs(seed: int = 0) -> dict:
    key = jax.random.key(seed)
    ks = jax.random.split(key, 32)
    f32 = jnp.float32
    n_pages = PAST_LEN // PAGE_SIZE
    n_used = DEC_BATCH * n_pages
    n_phys = n_used + n_used // 4

    def nrm(k, shape, scale):
        return jax.random.normal(k, shape, f32) * scale

    G, S = SSM_GROUPS, SSM_STATE
    lo, hi = math.log(0.001), math.log(0.1)
    return {
        "x_prompt": nrm(ks[0], (BATCH, SEQ, D_MODEL), 1.0),
        "x_sample": nrm(ks[1], (DEC_BATCH, DEC_SEQ, D_MODEL), 1.0),
        "cache_k": nrm(ks[2], (DEPTH, n_phys, PAGE_SIZE, N_HEADS, HEAD_DIM), 1.0),
        "cache_v": nrm(ks[3], (DEPTH, n_phys, PAGE_SIZE, N_HEADS, HEAD_DIM), 1.0),
        "state_ssm_re": nrm(ks[4], (DEPTH, DEC_BATCH, G, S), 0.5),
        "state_ssm_im": nrm(ks[5], (DEPTH, DEC_BATCH, G, S), 0.5),
        "state_conv": nrm(ks[6], (DEPTH, DEC_BATCH, CONV_W - 1, D_FF), 1.0),
        "page_table": jax.random.permutation(ks[7], n_phys)[:n_used].reshape(DEC_BATCH, n_pages).astype(jnp.int32),
        "norm1_w": 1.0 + nrm(ks[8], (DEPTH, D_MODEL), 0.05),
        "w_in": nrm(ks[9], (DEPTH, D_MODEL, IN_WIDTH), D_MODEL ** -0.5),
        "q_norm_w": 1.0 + nrm(ks[10], (DEPTH, HEAD_DIM), 0.05),
        "k_norm_w": 1.0 + nrm(ks[11], (DEPTH, HEAD_DIM), 0.05),
        "lam_re": -0.5 + nrm(ks[12], (DEPTH, G, S), 0.01),
        "lam_im": math.pi * jnp.arange(S, dtype=f32) + nrm(ks[13], (DEPTH, G, S), 0.01),
        "log_dt": lo + jax.random.uniform(ks[14], (DEPTH, G), f32) * (hi - lo),
        "b_re": nrm(ks[15], (DEPTH, G, S, SSM_GROUP), SSM_GROUP ** -0.5),
        "b_im": nrm(ks[16], (DEPTH, G, S, SSM_GROUP), SSM_GROUP ** -0.5),
        "c_re": nrm(ks[17], (DEPTH, G, SSM_GROUP, S), S ** -0.5),
        "c_im": nrm(ks[18], (DEPTH, G, SSM_GROUP, S), S ** -0.5),
        "d_skip": nrm(ks[19], (DEPTH, SSM_WIDTH), 1.0),
        "w_glu": nrm(ks[20], (DEPTH, SSM_WIDTH, SSM_WIDTH), SSM_WIDTH ** -0.5),
        "b_glu": nrm(ks[21], (DEPTH, SSM_WIDTH), 0.01),
        "w_a": nrm(ks[22], (DEPTH, SSM_WIDTH, D_MODEL), SSM_WIDTH ** -0.5),
        "w_b": nrm(ks[23], (DEPTH, ATT_WIDTH, D_MODEL), ATT_WIDTH ** -0.5),
        "w_o": nrm(ks[24], (DEPTH, D_MODEL, D_MODEL), D_MODEL ** -0.5),
        "norm2_w": 1.0 + nrm(ks[25], (DEPTH, D_MODEL), 0.05),
        "w_up": nrm(ks[26], (DEPTH, D_MODEL, D_FF), D_MODEL ** -0.5),
        "w_gate": nrm(ks[27], (DEPTH, D_MODEL, D_FF), D_MODEL ** -0.5),
        "conv_w": nrm(ks[28], (DEPTH, CONV_W, D_FF), CONV_W ** -0.5),
        "conv_b": nrm(ks[29], (DEPTH, D_FF), 0.01),
        "w_down": nrm(ks[30], (DEPTH, D_FF, D_MODEL), D_FF ** -0.5),
    }


def reference(x_prompt, x_sample, cache_k, cache_v, state_ssm_re, state_ssm_im, state_conv, page_table,
              norm1_w, w_in, q_norm_w, k_norm_w, lam_re, lam_im, log_dt, b_re, b_im, c_re, c_im, d_skip,
              w_glu, b_glu, w_a, w_b, w_o, norm2_w, w_up, w_gate, conv_w, conv_b, w_down):
    slopes = alibi_slopes()
    yp, ys = x_prompt, x_sample
    kp_l, vp_l, srp_l, sip_l, cp_l = [], [], [], [], []
    ks_l, vs_l, srs_l, sis_l, cs_l = [], [], [], [], []
    for l in range(DEPTH):
        ssm_w = (lam_re[l], lam_im[l], log_dt[l], b_re[l], b_im[l], c_re[l], c_im[l], d_skip[l], w_glu[l], b_glu[l])
        u, q, k, v, g_s, g_a = mixer_inputs(yp, norm1_w[l], w_in[l], q_norm_w[l], k_norm_w[l])
        h0 = jnp.zeros((yp.shape[0], SSM_GROUPS, SSM_STATE), jnp.float32)
        y_ssm, hr, hi = s5_branch(u, h0, h0, *ssm_w)
        y_att = moba_prompt(q, k, v, slopes)
        yp = gated_merge(yp, y_ssm, y_att, g_s, g_a, w_a[l], w_b[l], w_o[l])
        buf0 = jnp.zeros((yp.shape[0], CONV_W - 1, D_FF), yp.dtype)
        yp, cbuf = conv_ffn(yp, buf0, norm2_w[l], w_up[l], w_gate[l], conv_w[l], conv_b[l], w_down[l])
        kp_l.append(k); vp_l.append(v); srp_l.append(hr); sip_l.append(hi); cp_l.append(cbuf)
        u, q, k, v, g_s, g_a = mixer_inputs(ys, norm1_w[l], w_in[l], q_norm_w[l], k_norm_w[l])
        y_ssm, hr, hi = s5_branch(u, state_ssm_re[l], state_ssm_im[l], *ssm_w)
        y_att = moba_sample(q, k, v, cache_k[l], cache_v[l], page_table, slopes)
        ys = gated_merge(ys, y_ssm, y_att, g_s, g_a, w_a[l], w_b[l], w_o[l])
        ys, cbuf = conv_ffn(ys, state_conv[l], norm2_w[l], w_up[l], w_gate[l], conv_w[l], conv_b[l], w_down[l])
        ks_l.append(k); vs_l.append(v); srs_l.append(hr); sis_l.append(hi); cs_l.append(cbuf)
    return (yp, ys,
            jnp.stack(kp_l), jnp.stack(vp_l), jnp.stack(srp_l), jnp.stack(sip_l), jnp.stack(cp_l),
            jnp.stack(ks_l), jnp.stack(vs_l), jnp.stack(srs_l), jnp.stack(sis_l), jnp.stack(cs_l))
```

```python
import functools
import math

import jax
import jax.numpy as jnp
from jax import lax
from jax.experimental import pallas as pl
from jax.experimental.pallas import tpu as pltpu

F32 = jnp.float32
BF16 = jnp.bfloat16

D_MODEL = 1024
SSM_WIDTH = 512
SSM_GROUP = 16
SSM_GROUPS = 32
SSM_STATE = 64
N_STATE = SSM_GROUPS * SSM_STATE
N_HEADS = 8
HEAD_DIM = 64
ATT_WIDTH = 512
MOBA_BLOCK = 256
MOBA_TOPK = 3
D_FF = 2816
NORM_EPS = 1e-6
PAGE_SIZE = 128
LOG2E = 1.4426950408889634
NEG = -1e30

VMEM_LIMIT_BYTES = 56 * 1024 * 1024
SUBLANES = 8
LANES = 128


def _cparams(*sem):
    return pltpu.CompilerParams(dimension_semantics=sem, vmem_limit_bytes=VMEM_LIMIT_BYTES)


def _dot(a, b):
    return jnp.dot(a, b, preferred_element_type=F32)


def _dot_nt(a, b):
    return lax.dot_general(a, b, (((1,), (1,)), ((), ())), preferred_element_type=F32)


def _split_bf16(a):
    hi = a.astype(BF16)
    lo = (a - hi.astype(F32)).astype(BF16)
    return hi, lo


def _inproj_kernel(x_ref, n1_ref, win_ref, qn_ref, kn_ref, seg_ref,
                   u_ref, q_ref, k_ref, v_ref, gs_ref, ga_ref, kb_ref, vb_ref, *maybe_kmean,
                   tm):
    x = x_ref[...]
    ms = jnp.mean(x * x, axis=-1, keepdims=True)
    h = (x * lax.rsqrt(ms + NORM_EPS) * n1_ref[...]).astype(BF16)

    def proj(lo, hi):
        return _dot(h, win_ref[:, lo:hi])

    def head_rmsnorm(z, w_row):
        sq_hi, sq_lo = _split_bf16(z * z)
        ss = _dot(sq_hi, seg_ref[...]) + _dot(sq_lo, seg_ref[...])
        return z * lax.rsqrt(ss * (1.0 / HEAD_DIM) + NORM_EPS) * w_row

    o1 = SSM_WIDTH
    o2 = o1 + ATT_WIDTH
    o3 = o2 + ATT_WIDTH
    o4 = o3 + ATT_WIDTH
    o5 = o4 + D_MODEL
    u_ref[...] = proj(0, o1)
    q_ref[...] = head_rmsnorm(proj(o1, o2), qn_ref[...])
    k = head_rmsnorm(proj(o2, o3), kn_ref[...])
    k_ref[...] = k
    kb_ref[...] = k.astype(BF16)
    v = proj(o3, o4)
    v_ref[...] = v
    vb_ref[...] = v.astype(BF16)
    gs_ref[...] = proj(o4, o5)
    ga_ref[...] = proj(o5, o5 + D_MODEL)
    if maybe_kmean:
        (kmean_ref,) = maybe_kmean
        nb = tm // MOBA_BLOCK
        ksum = k.reshape(nb, MOBA_BLOCK, ATT_WIDTH).sum(axis=1)
        kmean_ref[...] = (ksum * (1.0 / MOBA_BLOCK)).reshape(nb, 1, ATT_WIDTH)


def _inproj(x2d, n1, win_b, qn_row, kn_row, seg, tm, with_kmean):
    m = x2d.shape[0]
    grid = (m // tm,)
    row = lambda w: pl.BlockSpec((tm, w), lambda i: (i, 0))
    full = lambda a: pl.BlockSpec(a.shape, lambda i: (0,) * a.ndim)
    out_shape = [jax.ShapeDtypeStruct((m, SSM_WIDTH), F32),
                 jax.ShapeDtypeStruct((m, ATT_WIDTH), F32),
                 jax.ShapeDtypeStruct((m, ATT_WIDTH), F32),
                 jax.ShapeDtypeStruct((m, ATT_WIDTH), F32),
                 jax.ShapeDtypeStruct((m, D_MODEL), F32),
                 jax.ShapeDtypeStruct((m, D_MODEL), F32),
                 jax.ShapeDtypeStruct((m, ATT_WIDTH), BF16),
                 jax.ShapeDtypeStruct((m, ATT_WIDTH), BF16)]
    out_specs = [row(SSM_WIDTH), row(ATT_WIDTH), row(ATT_WIDTH), row(ATT_WIDTH),
                 row(D_MODEL), row(D_MODEL), row(ATT_WIDTH), row(ATT_WIDTH)]
    if with_kmean:
        nb = tm // MOBA_BLOCK
        out_shape.append(jax.ShapeDtypeStruct((m // MOBA_BLOCK, 1, ATT_WIDTH), F32))
        out_specs.append(pl.BlockSpec((nb, 1, ATT_WIDTH), lambda i: (i, 0, 0)))
    return pl.pallas_call(
        functools.partial(_inproj_kernel, tm=tm),
        grid=grid,
        in_specs=[row(D_MODEL), full(n1), full(win_b), full(qn_row), full(kn_row), full(seg)],
        out_specs=out_specs,
        out_shape=out_shape,
        compiler_params=_cparams("arbitrary"),
        name="inproj",
    )(x2d, n1, win_b, qn_row, kn_row, seg)


def _s5_output(y_lin, u, dskip_ref, wglu_ref, bglu_ref):
    y = jax.nn.gelu(y_lin + dskip_ref[...] * u)
    gate = jax.nn.sigmoid(_dot(y.astype(BF16), wglu_ref[...]) + bglu_ref[...])
    return y * gate


S5_LANE_CHUNK = 512


def _s5_prompt_kernel(u_ref, bblk_ref, cblk_ref, tabs_ref, dskip_ref, wglu_ref, bglu_ref,
                      y_ref, hre_ref, him_ref, xre, xim, cre, cim, *, tt):
    t = pl.program_id(1)

    @pl.when(t == 0)
    def _():
        cre[...] = jnp.zeros_like(cre)
        cim[...] = jnp.zeros_like(cim)

    u = u_ref[...]
    ub = u.astype(BF16)
    xre[...] = _dot(ub, bblk_ref[:, :N_STATE])
    xim[...] = _dot(ub, bblk_ref[:, N_STATE:])

    for c in range(N_STATE // S5_LANE_CHUNK):
        ls = slice(c * S5_LANE_CHUNK, (c + 1) * S5_LANE_CHUNK)

        def body(b, carry, ls=ls):
            cr, ci = carry
            r0 = pl.multiple_of(b * SUBLANES, SUBLANES)
            xr = xre[pl.ds(r0, SUBLANES), ls]
            xi = xim[pl.ds(r0, SUBLANES), ls]
            for k, s in enumerate((1, 2, 4)):
                pr = tabs_ref[2 * k, :, ls]
                pi = tabs_ref[2 * k + 1, :, ls]
                rr = pltpu.roll(xr, s, 0)
                ri = pltpu.roll(xi, s, 0)
                xr, xi = xr + (pr * rr - pi * ri), xi + (pr * ri + pi * rr)
            pr = tabs_ref[6, :, ls]
            pi = tabs_ref[7, :, ls]
            hr = xr + (pr * cr - pi * ci)
            hi = xi + (pr * ci + pi * cr)
            xre[pl.ds(r0, SUBLANES), ls] = hr
            xim[pl.ds(r0, SUBLANES), ls] = hi
            shape = (SUBLANES, S5_LANE_CHUNK)
            return (jnp.broadcast_to(hr[SUBLANES - 1:SUBLANES, :], shape),
                    jnp.broadcast_to(hi[SUBLANES - 1:SUBLANES, :], shape))

        cr, ci = lax.fori_loop(0, tt // SUBLANES, body, (cre[:, ls], cim[:, ls]))
        cre[:, ls] = cr
        cim[:, ls] = ci

    hre_ref[0] = cre[0:1, :]
    him_ref[0] = cim[0:1, :]
    y_lin = (_dot(xre[...].astype(BF16), cblk_ref[:N_STATE, :])
             + _dot(xim[...].astype(BF16), cblk_ref[N_STATE:, :]))
    y_ref[...] = _s5_output(y_lin, u, dskip_ref, wglu_ref, bglu_ref)


def _s5_prompt(u2d, bblk, cblk, tabs, dskip, wglu_b, bglu, n_seq, t_len, tt):
    nt = t_len // tt
    full = lambda a: pl.BlockSpec(a.shape, lambda b, t: (0,) * a.ndim)
    return pl.pallas_call(
        functools.partial(_s5_prompt_kernel, tt=tt),
        grid=(n_seq, nt),
        in_specs=[pl.BlockSpec((tt, SSM_WIDTH), lambda b, t: (b * nt + t, 0)),
                  full(bblk), full(cblk), full(tabs), full(dskip), full(wglu_b), full(bglu)],
        out_specs=[pl.BlockSpec((tt, SSM_WIDTH), lambda b, t: (b * nt + t, 0)),
                   pl.BlockSpec((1, 1, N_STATE), lambda b, t: (b, 0, 0)),
                   pl.BlockSpec((1, 1, N_STATE), lambda b, t: (b, 0, 0))],
        out_shape=[jax.ShapeDtypeStruct((n_seq * t_len, SSM_WIDTH), F32),
                   jax.ShapeDtypeStruct((n_seq, 1, N_STATE), F32),
                   jax.ShapeDtypeStruct((n_seq, 1, N_STATE), F32)],
        scratch_shapes=[pltpu.VMEM((tt, N_STATE), F32), pltpu.VMEM((tt, N_STATE), F32),
                        pltpu.VMEM((SUBLANES, N_STATE), F32), pltpu.VMEM((SUBLANES, N_STATE), F32)],
        compiler_params=_cparams("arbitrary", "arbitrary"),
        name="s5_prompt",
    )(u2d, bblk, cblk, tabs, dskip, wglu_b, bglu)


def _s5_step_kernel(u_ref, h0re_ref, h0im_ref, are_ref, aim_ref, bblk_ref, cblk_ref,
                    dskip_ref, wglu_ref, bglu_ref, y_ref, hre_ref, him_ref):
    u = u_ref[...]
    ub = u.astype(BF16)
    ar = are_ref[...]
    ai = aim_ref[...]
    h0r = h0re_ref[...]
    h0i = h0im_ref[...]
    hr = _dot(ub, bblk_ref[:, :N_STATE]) + (ar * h0r - ai * h0i)
    hi = _dot(ub, bblk_ref[:, N_STATE:]) + (ar * h0i + ai * h0r)
    hre_ref[...] = hr
    him_ref[...] = hi
    y_lin = _dot(hr.astype(BF16), cblk_ref[:N_STATE, :]) + _dot(hi.astype(BF16), cblk_ref[N_STATE:, :])
    y_ref[...] = _s5_output(y_lin, u, dskip_ref, wglu_ref, bglu_ref)


def _s5_step(u2d, h0re, h0im, are, aim, bblk, cblk, dskip, wglu_b, bglu):
    m = u2d.shape[0]
    args = (u2d, h0re, h0im, are, aim, bblk, cblk, dskip, wglu_b, bglu)
    full = lambda a: pl.BlockSpec(a.shape, lambda i: (0,) * a.ndim)
    return pl.pallas_call(
        _s5_step_kernel,
        grid=(1,),
        in_specs=[full(a) for a in args],
        out_specs=[pl.BlockSpec((m, SSM_WIDTH), lambda i: (0, 0)),
                   pl.BlockSpec((m, N_STATE), lambda i: (0, 0)),
                   pl.BlockSpec((m, N_STATE), lambda i: (0, 0))],
        out_shape=[jax.ShapeDtypeStruct((m, SSM_WIDTH), F32),
                   jax.ShapeDtypeStruct((m, N_STATE), F32),
                   jax.ShapeDtypeStruct((m, N_STATE), F32)],
        compiler_params=_cparams("arbitrary"),
        name="s5_step",
    )(*args)


def _topk_bias(gate, nblk):
    bidx = lax.broadcasted_iota(jnp.int32, gate.shape, 1).astype(F32)
    bias = jnp.full(gate.shape, NEG, F32)
    g = gate
    for _ in range(MOBA_TOPK):
        mx = jnp.max(g, axis=-1, keepdims=True)
        am = jnp.min(jnp.where(g == mx, bidx, float(nblk)), axis=-1, keepdims=True)
        hit = bidx == am
        bias = jnp.where(hit & (mx > -jnp.inf), 0.0, bias)
        g = jnp.where(hit, -jnp.inf, g)
    return bias


def _moba_prompt_kernel(q_ref, kb_ref, vb_ref, kmean_ref, slope_ref, o_ref, *, nblk):
    i = pl.program_id(1)
    bq = MOBA_BLOCK
    lane = lax.broadcasted_iota(jnp.int32, (1, LANES), 1)
    row = lax.broadcasted_iota(jnp.int32, (bq, bq), 0)
    col = lax.broadcasted_iota(jnp.int32, (bq, bq), 1)
    colf = lax.broadcasted_iota(jnp.int32, (1, bq), 1).astype(F32)
    blk_row = lax.broadcasted_iota(jnp.int32, (nblk, bq), 0)
    gidx = lax.broadcasted_iota(jnp.int32, (bq, nblk), 1)

    for p in range(N_HEADS // 2):
        ls = slice(p * LANES, (p + 1) * LANES)
        qp = q_ref[0, :, ls]
        km_hi, km_lo = _split_bf16(kmean_ref[0, :, ls])
        out_pair = jnp.zeros((bq, LANES), F32)
        for hh in range(2):
            h = 2 * p + hh
            own = (lane // HEAD_DIM) == hh
            qh = jnp.where(own, qp, 0.0)
            q_hi, q_lo = _split_bf16(qh)
            gate = _dot_nt(q_hi, km_hi) + _dot_nt(q_hi, km_lo) + _dot_nt(q_lo, km_hi)
            gate = jnp.where(gidx < i, gate, -jnp.inf)
            selb = _topk_bias(gate, nblk).astype(BF16)
            slope2 = slope_ref[h] * LOG2E
            qs = (qh * (HEAD_DIM ** -0.5 * LOG2E)).astype(BF16)

            r0 = pl.multiple_of(i * bq, bq)
            s = _dot_nt(qs, kb_ref[0, pl.ds(r0, bq), ls]) + slope2 * colf
            s = jnp.where(col <= row, s, NEG)
            m = jnp.max(s, axis=-1, keepdims=True)
            pexp = jnp.exp2(s - m)
            l = jnp.sum(pexp, axis=-1, keepdims=True)
            acc = _dot(pexp.astype(BF16), vb_ref[0, pl.ds(r0, bq), ls])

            def body(j, carry, ls=ls, qs=qs, selb=selb, slope2=slope2):
                m, l, acc = carry
                c0 = pl.multiple_of(j * bq, bq)
                onehot = (blk_row == j).astype(BF16)
                s = (_dot_nt(qs, kb_ref[0, pl.ds(c0, bq), ls]) + _dot(selb, onehot)
                     + slope2 * (colf + ((j - i) * bq).astype(F32)))
                m_new = jnp.maximum(m, jnp.max(s, axis=-1, keepdims=True))
                alpha = jnp.exp2(m - m_new)
                pexp = jnp.exp2(s - m_new)
                l = alpha * l + jnp.sum(pexp, axis=-1, keepdims=True)
                acc = alpha * acc + _dot(pexp.astype(BF16), vb_ref[0, pl.ds(c0, bq), ls])
                return m_new, l, acc

            m, l, acc = lax.fori_loop(0, i, body, (m, l, acc))
            out_pair = jnp.where(own, acc / l, out_pair)
        o_ref[0, :, ls] = out_pair


def _moba_prompt(q3, kb3, vb3, kmean3, slopes):
    n, t_len, _ = q3.shape
    nblk = t_len // MOBA_BLOCK
    return pl.pallas_call(
        functools.partial(_moba_prompt_kernel, nblk=nblk),
        grid=(n, nblk),
        in_specs=[pl.BlockSpec((1, MOBA_BLOCK, ATT_WIDTH), lambda b, i: (b, i, 0)),
                  pl.BlockSpec((1, t_len, ATT_WIDTH), lambda b, i: (b, 0, 0)),
                  pl.BlockSpec((1, t_len, ATT_WIDTH), lambda b, i: (b, 0, 0)),
                  pl.BlockSpec((1, nblk, ATT_WIDTH), lambda b, i: (b, 0, 0)),
                  pl.BlockSpec(memory_space=pltpu.SMEM)],
        out_specs=pl.BlockSpec((1, MOBA_BLOCK, ATT_WIDTH), lambda b, i: (b, i, 0)),
        out_shape=jax.ShapeDtypeStruct((n, t_len, ATT_WIDTH), F32),
        compiler_params=_cparams("arbitrary", "arbitrary"),
        name="moba_prompt",
    )(q3, kb3, vb3, kmean3, slopes)


PAGES_PER_STEP = 16


def _pagesum_kernel(pt_ref, *refs):
    page_refs, out_ref = refs[:PAGES_PER_STEP], refs[PAGES_PER_STEP]
    for r in range(0, PAGES_PER_STEP, 2):
        s = jnp.sum(page_refs[r][0, 0], axis=0) + jnp.sum(page_refs[r + 1][0, 0], axis=0)
        out_ref[0, r // 2] = s


def _pagesum(cache_k, page_table):
    n, n_pages = page_table.shape
    steps = n_pages // PAGES_PER_STEP
    blocks_per_step = PAGES_PER_STEP * PAGE_SIZE // MOBA_BLOCK

    def page_spec(r):
        return pl.BlockSpec((1, 1, PAGE_SIZE, N_HEADS, HEAD_DIM),
                            lambda b, c, pt: (0, pt[b, c * PAGES_PER_STEP + r], 0, 0, 0))

    return pl.pallas_call(
        _pagesum_kernel,
        grid_spec=pltpu.PrefetchScalarGridSpec(
            num_scalar_prefetch=1,
            grid=(n, steps),
            in_specs=[page_spec(r) for r in range(PAGES_PER_STEP)],
            out_specs=pl.BlockSpec((1, blocks_per_step, N_HEADS, HEAD_DIM), lambda b, c, pt: (b, c, 0, 0)),
        ),
        out_shape=jax.ShapeDtypeStruct((n, n_pages * PAGE_SIZE // MOBA_BLOCK, N_HEADS, HEAD_DIM), F32),
        compiler_params=_cparams("arbitrary", "arbitrary"),
        name="pagesum",
    )(page_table, *([cache_k] * PAGES_PER_STEP))


def _sample_topk_kernel(q_ref, bsum_ref, knew_ref, idx_ref, *, n_past):
    q = q_ref[...]
    means = jnp.concatenate([bsum_ref[...], knew_ref[...]], axis=1) * (1.0 / MOBA_BLOCK)
    gate = jnp.sum(means * q, axis=-1, keepdims=True)
    bidx = lax.broadcasted_iota(jnp.int32, gate.shape, 1)
    g = jnp.where(bidx < n_past, gate, -jnp.inf)
    for k in range(MOBA_TOPK):
        mx = jnp.max(g, axis=1, keepdims=True)
        am = jnp.min(jnp.where(g == mx, bidx, n_past + 1), axis=1, keepdims=True)
        idx_ref[:, k:k + 1] = jnp.broadcast_to(am, am.shape[:3] + (LANES,))
        g = jnp.where(bidx == am, -jnp.inf, g)


def _sample_topk(q4, bsum, knew4, nb=16):
    n, n_past = bsum.shape[:2]
    blk = lambda s1: pl.BlockSpec((nb, s1, N_HEADS, HEAD_DIM), lambda i: (i, 0, 0, 0))
    return pl.pallas_call(
        functools.partial(_sample_topk_kernel, n_past=n_past),
        grid=(n // nb,),
        in_specs=[blk(1), blk(n_past), blk(1)],
        out_specs=pl.BlockSpec((nb, MOBA_TOPK, N_HEADS, LANES), lambda i: (i, 0, 0, 0)),
        out_shape=jax.ShapeDtypeStruct((n, MOBA_TOPK, N_HEADS, LANES), jnp.int32),
        compiler_params=_cparams("arbitrary"),
        name="sample_topk",
    )(q4, bsum, knew4)


PAGES_PER_BLOCK = MOBA_BLOCK // PAGE_SIZE
SAMPLE_STEPS_PER_HEAD = MOBA_TOPK * PAGES_PER_BLOCK


def _moba_sample_kernel(pt_ref, idx_ref, slope_ref, q_ref, knew_ref, vnew_ref, kpage_ref, vpage_ref,
                        o_ref, m_sc, l_sc, acc_sc, *, past_len):
    b = pl.program_id(0)
    r = pl.program_id(1)
    h = r // SAMPLE_STEPS_PER_HEAD
    w = r % SAMPLE_STEPS_PER_HEAD
    slot = w // PAGES_PER_BLOCK
    half = w % PAGES_PER_BLOCK
    slope = slope_ref[h]
    scale = HEAD_DIM ** -0.5
    qh = q_ref[0, pl.ds(h, 1), :]

    @pl.when(w == 0)
    def _():
        s0 = jnp.sum(qh * knew_ref[0, pl.ds(h, 1), :], axis=-1, keepdims=True) * scale
        m_sc[...] = s0
        l_sc[...] = jnp.ones_like(s0)
        acc_sc[...] = vnew_ref[0, pl.ds(h, 1), :]

    kh = kpage_ref[0, 0, :, pl.ds(h, 1), :]
    vh = vpage_ref[0, 0, :, pl.ds(h, 1), :]
    pos0 = idx_ref[b, slot, h] * MOBA_BLOCK + half * PAGE_SIZE
    off = lax.broadcasted_iota(jnp.int32, (PAGE_SIZE, 1, 1), 0)
    dist = (past_len - pos0 - off).astype(F32)
    s = jnp.sum(kh * qh[None], axis=-1, keepdims=True) * scale - slope * dist
    m_old = m_sc[...].reshape(1, 1, 1)
    m_new = jnp.maximum(m_old, jnp.max(s, axis=0, keepdims=True))
    alpha = jnp.exp(m_old - m_new)
    pexp = jnp.exp(s - m_new)
    l_new = alpha * l_sc[...].reshape(1, 1, 1) + jnp.sum(pexp, axis=0, keepdims=True)
    acc = alpha * acc_sc[...].reshape(1, 1, HEAD_DIM) + jnp.sum(pexp * vh, axis=0, keepdims=True)
    m_sc[...] = m_new.reshape(1, 1)
    l_sc[...] = l_new.reshape(1, 1)
    acc_sc[...] = acc.reshape(1, HEAD_DIM)

    @pl.when(w == SAMPLE_STEPS_PER_HEAD - 1)
    def _():
        o_ref[0, pl.ds(h, 1), :] = acc.reshape(1, HEAD_DIM) / l_new.reshape(1, 1)


def _moba_sample(page_table, idx, slopes, q3, knew3, vnew3, cache_k, cache_v, past_len):
    n = q3.shape[0]

    def page_map(b, r, pt, ix, sl):
        h = r // SAMPLE_STEPS_PER_HEAD
        w = r % SAMPLE_STEPS_PER_HEAD
        page = ix[b, w // PAGES_PER_BLOCK, h] * PAGES_PER_BLOCK + w % PAGES_PER_BLOCK
        return (0, pt[b, page], 0, 0, 0)

    tok = pl.BlockSpec((1, N_HEADS, HEAD_DIM), lambda b, r, pt, ix, sl: (b, 0, 0))
    page = pl.BlockSpec((1, 1, PAGE_SIZE, N_HEADS, HEAD_DIM), page_map)
    return pl.pallas_call(
        functools.partial(_moba_sample_kernel, past_len=past_len),
        grid_spec=pltpu.PrefetchScalarGridSpec(
            num_scalar_prefetch=3,
            grid=(n, N_HEADS * SAMPLE_STEPS_PER_HEAD),
            in_specs=[tok, tok, tok, page, page],
            out_specs=tok,
            scratch_shapes=[pltpu.VMEM((1, 1), F32), pltpu.VMEM((1, 1), F32), pltpu.VMEM((1, HEAD_DIM), F32)],
        ),
        out_shape=jax.ShapeDtypeStruct((n, N_HEADS, HEAD_DIM), F32),
        compiler_params=_cparams("arbitrary", "arbitrary"),
        name="moba_sample",
    )(page_table, idx, slopes, q3, knew3, vnew3, cache_k, cache_v)


def _merge_kernel(x_ref, ys_ref, ya_ref, gs_ref, ga_ref, wa_ref, wb_ref, wo_ref, o_ref):
    a = _dot(ys_ref[...].astype(BF16), wa_ref[...])
    b = _dot(ya_ref[...].astype(BF16), wb_ref[...])
    mix = jax.nn.sigmoid(gs_ref[...]) * a + jax.nn.sigmoid(ga_ref[...]) * b
    o_ref[...] = x_ref[...] + _dot(mix.astype(BF16), wo_ref[...])


def _merge(x2d, ys, ya, gs, ga, wa_b, wb_b, wo_b, tm):
    m = x2d.shape[0]
    row = lambda w: pl.BlockSpec((tm, w), lambda i: (i, 0))
    full = lambda a: pl.BlockSpec(a.shape, lambda i: (0,) * a.ndim)
    return pl.pallas_call(
        _merge_kernel,
        grid=(m // tm,),
        in_specs=[row(D_MODEL), row(SSM_WIDTH), row(ATT_WIDTH), row(D_MODEL), row(D_MODEL),
                  full(wa_b), full(wb_b), full(wo_b)],
        out_specs=row(D_MODEL),
        out_shape=jax.ShapeDtypeStruct((m, D_MODEL), F32),
        compiler_params=_cparams("arbitrary"),
        name="merge",
    )(x2d, ys, ya, gs, ga, wa_b, wb_b, wo_b)


FF_CHUNK = D_FF // 2


def _ffn_kernel(x_ref, *refs, tm, sequential):
    if sequential:
        (n2_ref, wup_ref, wgate_ref, cw_ref, cb_ref, wdown_ref, o_ref, cs_ref, tail) = refs
    else:
        (buf0_ref, buf1_ref, n2_ref, wup_ref, wgate_ref, cw_ref, cb_ref, wdown_ref, o_ref, cs0_ref, cs1_ref) = refs
    x = x_ref[...]
    ms = jnp.mean(x * x, axis=-1, keepdims=True)
    h = (x * lax.rsqrt(ms + NORM_EPS) * n2_ref[...]).astype(BF16)

    if sequential:
        @pl.when(pl.program_id(1) == 0)
        def _():
            tail[...] = jnp.zeros_like(tail)
        rows = lax.broadcasted_iota(jnp.int32, (tm, 1), 0)

    y = x
    for c in range(D_FF // FF_CHUNK):
        cs = slice(c * FF_CHUNK, (c + 1) * FF_CHUNK)
        up = _dot(h, wup_ref[:, cs])
        g = _dot(h, wgate_ref[:, cs])
        if sequential:
            p2 = tail[SUBLANES - 2:SUBLANES - 1, cs]
            p1 = tail[SUBLANES - 1:SUBLANES, cs]
            g1 = jnp.where(rows == 0, p1, pltpu.roll(g, 1, 0))
            g2 = jnp.where(rows == 0, p2, jnp.where(rows == 1, p1, pltpu.roll(g, 2, 0)))
            tail[:, cs] = g[tm - SUBLANES:, :]
            cs_ref[0, :, cs] = g[tm - 2:, :]
        else:
            g2 = buf0_ref[:, cs]
            g1 = buf1_ref[:, cs]
            cs0_ref[:, cs] = g1
            cs1_ref[:, cs] = g
        conv = cb_ref[:, cs] + cw_ref[0:1, cs] * g2 + cw_ref[1:2, cs] * g1 + cw_ref[2:3, cs] * g
        act = (jax.nn.gelu(conv) * up).astype(BF16)
        y = y + _dot(act, wdown_ref[cs, :])
    o_ref[...] = y


def _ffn_prompt(x2d, n2, wup_b, wgate_b, cw, cb, wdown_b, n_seq, t_len, tm):
    nt = t_len // tm
    full = lambda a: pl.BlockSpec(a.shape, lambda b, t: (0,) * a.ndim)
    row = pl.BlockSpec((tm, D_MODEL), lambda b, t: (b * nt + t, 0))
    return pl.pallas_call(
        functools.partial(_ffn_kernel, tm=tm, sequential=True),
        grid=(n_seq, nt),
        in_specs=[row, full(n2), full(wup_b), full(wgate_b), full(cw), full(cb), full(wdown_b)],
        out_specs=[row, pl.BlockSpec((1, 2, D_FF), lambda b, t: (b, 0, 0))],
        out_shape=[jax.ShapeDtypeStruct((n_seq * t_len, D_MODEL), F32),
                   jax.ShapeDtypeStruct((n_seq, 2, D_FF), F32)],
        scratch_shapes=[pltpu.VMEM((SUBLANES, D_FF), F32)],
        compiler_params=_cparams("arbitrary", "arbitrary"),
        name="ffn_prompt",
    )(x2d, n2, wup_b, wgate_b, cw, cb, wdown_b)


def _ffn_step(x2d, buf0, buf1, n2, wup_b, wgate_b, cw, cb, wdown_b):
    m = x2d.shape[0]
    args = (x2d, buf0, buf1, n2, wup_b, wgate_b, cw, cb, wdown_b)
    full = lambda a: pl.BlockSpec(a.shape, lambda i: (0,) * a.ndim)
    return pl.pallas_call(
        functools.partial(_ffn_kernel, tm=m, sequential=False),
        grid=(1,),
        in_specs=[full(a) for a in args],
        out_specs=[pl.BlockSpec((m, D_MODEL), lambda i: (0, 0)),
                   pl.BlockSpec((m, D_FF), lambda i: (0, 0)),
                   pl.BlockSpec((m, D_FF), lambda i: (0, 0))],
        out_shape=[jax.ShapeDtypeStruct((m, D_MODEL), F32),
                   jax.ShapeDtypeStruct((m, D_FF), F32),
                   jax.ShapeDtypeStruct((m, D_FF), F32)],
        compiler_params=_cparams("arbitrary"),
        name="ffn_step",
    )(*args)


def _s5_params(lam_re, lam_im, log_dt, b_re, b_im, c_re, c_im):
    lr, li = lam_re.astype(F32), lam_im.astype(F32)
    dt = jnp.exp(log_dt.astype(F32))[:, None]
    mag = jnp.exp(lr * dt)
    ar, ai = mag * jnp.cos(li * dt), mag * jnp.sin(li * dt)
    den = lr * lr + li * li
    fr = ((ar - 1.0) * lr + ai * li) / den
    fi = (ai * lr - (ar - 1.0) * li) / den
    br_, bi_ = b_re.astype(F32), b_im.astype(F32)
    bbr = fr[..., None] * br_ - fi[..., None] * bi_
    bbi = fr[..., None] * bi_ + fi[..., None] * br_
    eye = jnp.eye(SSM_GROUPS, dtype=F32)

    def bdiag_in(w):
        return jnp.einsum('gsc,gh->gchs', w, eye).reshape(SSM_WIDTH, N_STATE)

    def bdiag_out(w):
        return jnp.einsum('gcs,gh->gshc', w, eye).reshape(N_STATE, SSM_WIDTH)

    bblk = jnp.concatenate([bdiag_in(bbr), bdiag_in(bbi)], axis=1).astype(BF16)
    cblk = jnp.concatenate([bdiag_out(c_re.astype(F32)), -bdiag_out(c_im.astype(F32))], axis=0).astype(BF16)

    a_re, a_im = ar.reshape(1, N_STATE), ai.reshape(1, N_STATE)

    def cmul(x, y):
        return x[0] * y[0] - x[1] * y[1], x[0] * y[1] + x[1] * y[0]

    pows = [(a_re, a_im)]
    for _ in range(SUBLANES - 1):
        pows.append(cmul(pows[-1], (a_re, a_im)))
    rows = jnp.arange(SUBLANES)[:, None]
    tabs = []
    for s in (1, 2, 4):
        tabs.append(jnp.where(rows >= s, pows[s - 1][0], 0.0))
        tabs.append(jnp.where(rows >= s, pows[s - 1][1], 0.0))
    tabs.append(jnp.concatenate([p[0] for p in pows], axis=0))
    tabs.append(jnp.concatenate([p[1] for p in pows], axis=0))
    return a_re, a_im, bblk, cblk, jnp.stack(tabs)


def kernel(x_prompt, x_sample, cache_k, cache_v, state_ssm_re, state_ssm_im, state_conv, page_table,
           norm1_w, w_in, q_norm_w, k_norm_w, lam_re, lam_im, log_dt, b_re, b_im, c_re, c_im, d_skip,
           w_glu, b_glu, w_a, w_b, w_o, norm2_w, w_up, w_gate, conv_w, conv_b, w_down):
    depth = norm1_w.shape[0]
    batch, seq, _ = x_prompt.shape
    dec_batch, dec_seq, _ = x_sample.shape
    assert depth == 1 and dec_seq == 1 and seq % MOBA_BLOCK == 0
    past_len = page_table.shape[1] * PAGE_SIZE
    slopes = 2.0 ** (-8.0 * jnp.arange(1, N_HEADS + 1, dtype=F32) / N_HEADS)
    seg = jnp.kron(jnp.eye(N_HEADS, dtype=F32), jnp.ones((HEAD_DIM, HEAD_DIM), F32)).astype(BF16)

    l = 0
    row = lambda a: a.reshape(1, -1).astype(F32)
    n1, n2 = row(norm1_w[l]), row(norm2_w[l])
    qn_row = row(jnp.tile(q_norm_w[l], N_HEADS))
    kn_row = row(jnp.tile(k_norm_w[l], N_HEADS))
    win_b = w_in[l].astype(BF16)
    a_re, a_im, bblk, cblk, tabs = _s5_params(lam_re[l], lam_im[l], log_dt[l], b_re[l], b_im[l], c_re[l], c_im[l])
    dskip, bglu = row(d_skip[l]), row(b_glu[l])
    wglu_b = w_glu[l].astype(BF16)
    wa_b, wb_b, wo_b = w_a[l].astype(BF16), w_b[l].astype(BF16), w_o[l].astype(BF16)
    wup_b, wgate_b, wdown_b = w_up[l].astype(BF16), w_gate[l].astype(BF16), w_down[l].astype(BF16)
    cw, cb = conv_w[l].astype(F32), row(conv_b[l])

    mp = batch * seq
    xp = x_prompt.reshape(mp, D_MODEL)
    u, q, k, v, gs, ga, kb, vb, kmean = _inproj(xp, n1, win_b, qn_row, kn_row, seg, 256, True)
    y_ssm, hre_p, him_p = _s5_prompt(u, bblk, cblk, tabs, dskip, wglu_b, bglu, batch, seq, 256)
    y_att = _moba_prompt(q.reshape(batch, seq, ATT_WIDTH), kb.reshape(batch, seq, ATT_WIDTH),
                         vb.reshape(batch, seq, ATT_WIDTH),
                         kmean.reshape(batch, seq // MOBA_BLOCK, ATT_WIDTH), slopes)
    x1 = _merge(xp, y_ssm, y_att.reshape(mp, ATT_WIDTH), gs, ga, wa_b, wb_b, wo_b, 512)
    yp, conv_p = _ffn_prompt(x1, n2, wup_b, wgate_b, cw, cb, wdown_b, batch, seq, 256)

    xs = x_sample.reshape(dec_batch, D_MODEL)
    u_s, q_s, k_s, v_s, gs_s, ga_s, _, _ = _inproj(xs, n1, win_b, qn_row, kn_row, seg, dec_batch, False)
    y_ssm_s, hre_s, him_s = _s5_step(u_s, state_ssm_re[l].reshape(dec_batch, N_STATE),
                                     state_ssm_im[l].reshape(dec_batch, N_STATE),
                                     a_re, a_im, bblk, cblk, dskip, wglu_b, bglu)
    q4 = q_s.reshape(dec_batch, 1, N_HEADS, HEAD_DIM)
    k4 = k_s.reshape(dec_batch, 1, N_HEADS, HEAD_DIM)
    bsum = _pagesum(cache_k, page_table)
    idx = _sample_topk(q4, bsum, k4)[..., 0]
    y_att_s = _moba_sample(page_table, idx, slopes, q4[:, 0], k4[:, 0],
                           v_s.reshape(dec_batch, N_HEADS, HEAD_DIM), cache_k, cache_v, past_len)
    x1_s = _merge(xs, y_ssm_s, y_att_s.reshape(dec_batch, ATT_WIDTH), gs_s, ga_s, wa_b, wb_b, wo_b, dec_batch)
    ys, cs0, cs1 = _ffn_step(x1_s, state_conv[l, :, 0, :], state_conv[l, :, 1, :], n2, wup_b, wgate_b, cw, cb, wdown_b)

    g, s = SSM_GROUPS, SSM_STATE
    return (yp.reshape(batch, seq, D_MODEL),
            ys.reshape(dec_batch, 1, D_MODEL),
            k.reshape(1, batch, seq, N_HEADS, HEAD_DIM),
            v.reshape(1, batch, seq, N_HEADS, HEAD_DIM),
            hre_p.reshape(1, batch, g, s),
            him_p.reshape(1, batch, g, s),
            conv_p.reshape(1, batch, 2, D_FF),
            k_s.reshape(1, dec_batch, 1, N_HEADS, HEAD_DIM),
            v_s.reshape(1, dec_batch, 1, N_HEADS, HEAD_DIM),
            hre_s.reshape(1, dec_batch, g, s),
            him_s.reshape(1, dec_batch, g, s),
            jnp.stack([cs0, cs1], axis=1).reshape(1, dec_batch, 2, D_FF))
```

```python
import functools

import jax
import jax.numpy as jnp
from jax import lax
from jax.experimental import pallas as pl
from jax.experimental.pallas import tpu as pltpu

F32 = jnp.float32
BF16 = jnp.bfloat16

D_MODEL = 1024
SSM_WIDTH = 512
SSM_GROUP = 16
SSM_GROUPS = 32
SSM_STATE = 64
N_STATE = SSM_GROUPS * SSM_STATE
N_HEADS = 8
HEAD_DIM = 64
ATT_WIDTH = 512
MOBA_BLOCK = 256
MOBA_TOPK = 3
D_FF = 2816
NORM_EPS = 1e-6
PAGE_SIZE = 128
LOG2E = 1.4426950408889634
NEG = -1e30

VMEM_LIMIT_BYTES = 56 * 1024 * 1024
SUBLANES = 8
LANES = 128


def _cparams(*sem):
    return pltpu.CompilerParams(dimension_semantics=sem, vmem_limit_bytes=VMEM_LIMIT_BYTES)


def _dot(a, b):
    return jnp.dot(a, b, preferred_element_type=F32)


def _dot_nt(a, b):
    return lax.dot_general(a, b, (((1,), (1,)), ((), ())), preferred_element_type=F32)


def _split_bf16(a):
    hi = a.astype(BF16)
    lo = (a - hi.astype(F32)).astype(BF16)
    return hi, lo


V_ROWS = 80


def _inproj_kernel(x_ref, n1_ref, win_ref, qn_ref, kn_ref, seg_ref, u_ref, q_ref, gs_ref, ga_ref, *rest,
                   tm, prompt):
    x = x_ref[...]
    ms = jnp.mean(x * x, axis=-1, keepdims=True)
    h = (x * lax.rsqrt(ms + NORM_EPS) * n1_ref[...]).astype(BF16)

    def proj(lo, hi):
        return _dot(h, win_ref[:, lo:hi])

    def head_rmsnorm(z, w_row):
        sq_hi, sq_lo = _split_bf16(z * z)
        ss = _dot(sq_hi, seg_ref[...]) + _dot(sq_lo, seg_ref[...])
        return z * lax.rsqrt(ss * (1.0 / HEAD_DIM) + NORM_EPS) * w_row

    o1 = SSM_WIDTH
    o2 = o1 + ATT_WIDTH
    o3 = o2 + ATT_WIDTH
    o4 = o3 + ATT_WIDTH
    o5 = o4 + D_MODEL
    u_ref[...] = proj(0, o1)
    q_ref[...] = head_rmsnorm(proj(o1, o2), qn_ref[...])
    k = head_rmsnorm(proj(o2, o3), kn_ref[...])
    v = proj(o3, o4)
    gs_ref[...] = proj(o4, o5)
    ga_ref[...] = proj(o5, o5 + D_MODEL)
    if not prompt:
        k_ref, v_ref = rest
        k_ref[...] = k
        v_ref[...] = v
        return
    kt_ref, vt_ref, kb_ref, vta_ref, kmean_ref = rest
    nb = tm // MOBA_BLOCK
    kt_ref[0] = k.T
    vt = v.T
    vt_ref[0] = vt
    kb_ref[...] = k.astype(BF16)
    ones = jnp.ones((V_ROWS - HEAD_DIM, MOBA_BLOCK), BF16)
    for hd in range(N_HEADS):
        for b in range(nb):
            vta_ref[0, b, hd * V_ROWS:hd * V_ROWS + HEAD_DIM, :] = (
                vt[hd * HEAD_DIM:(hd + 1) * HEAD_DIM, b * MOBA_BLOCK:(b + 1) * MOBA_BLOCK].astype(BF16))
            vta_ref[0, b, hd * V_ROWS + HEAD_DIM:(hd + 1) * V_ROWS, :] = ones
    ksum = k.reshape(nb, MOBA_BLOCK, ATT_WIDTH).sum(axis=1)
    kmean_ref[...] = (ksum * (1.0 / MOBA_BLOCK)).reshape(nb, 1, ATT_WIDTH)


def _inproj(x2d, n1, win_b, qn_row, kn_row, seg, n_seq, t_len, tm, prompt):
    nt = t_len // tm
    m = n_seq * t_len
    row = lambda w: pl.BlockSpec((tm, w), lambda b, t: (b * nt + t, 0))
    full = lambda a: pl.BlockSpec(a.shape, lambda b, t: (0,) * a.ndim)
    f32 = lambda *s: jax.ShapeDtypeStruct(s, F32)
    out_shape = [f32(m, SSM_WIDTH), f32(m, ATT_WIDTH), f32(m, D_MODEL), f32(m, D_MODEL)]
    out_specs = [row(SSM_WIDTH), row(ATT_WIDTH), row(D_MODEL), row(D_MODEL)]
    if prompt:
        nb = tm // MOBA_BLOCK
        tspec = pl.BlockSpec((1, ATT_WIDTH, tm), lambda b, t: (b, 0, t))
        out_shape += [f32(n_seq, ATT_WIDTH, t_len), f32(n_seq, ATT_WIDTH, t_len),
                      jax.ShapeDtypeStruct((m, ATT_WIDTH), BF16),
                      jax.ShapeDtypeStruct((n_seq, t_len // MOBA_BLOCK, N_HEADS * V_ROWS, MOBA_BLOCK), BF16),
                      f32(m // MOBA_BLOCK, 1, ATT_WIDTH)]
        out_specs += [tspec, tspec, row(ATT_WIDTH),
                      pl.BlockSpec((1, nb, N_HEADS * V_ROWS, MOBA_BLOCK), lambda b, t: (b, t, 0, 0)),
                      pl.BlockSpec((nb, 1, ATT_WIDTH), lambda b, t: (b * nt + t, 0, 0))]
    else:
        out_shape += [f32(m, ATT_WIDTH), f32(m, ATT_WIDTH)]
        out_specs += [row(ATT_WIDTH), row(ATT_WIDTH)]
    return pl.pallas_call(
        functools.partial(_inproj_kernel, tm=tm, prompt=prompt),
        grid=(n_seq, nt),
        in_specs=[row(D_MODEL), full(n1), full(win_b), full(qn_row), full(kn_row), full(seg)],
        out_specs=out_specs,
        out_shape=out_shape,
        compiler_params=_cparams("arbitrary", "arbitrary"),
        name="inproj",
    )(x2d, n1, win_b, qn_row, kn_row, seg)


def _s5_output(y_lin, u, dskip_ref, wglu_ref, bglu_ref):
    y = jax.nn.gelu(y_lin + dskip_ref[...] * u)
    gate = jax.nn.sigmoid(_dot(y.astype(BF16), wglu_ref[...]) + bglu_ref[...])
    return y * gate


S5_LANE_CHUNK = 512


def _s5_prompt_kernel(u_ref, bblk_ref, cblk_ref, tabs_ref, dskip_ref, wglu_ref, bglu_ref,
                      y_ref, hre_ref, him_ref, xre, xim, cre, cim, *, tt):
    t = pl.program_id(1)

    @pl.when(t == 0)
    def _():
        cre[...] = jnp.zeros_like(cre)
        cim[...] = jnp.zeros_like(cim)

    u = u_ref[...]
    ub = u.astype(BF16)
    xre[...] = _dot(ub, bblk_ref[:, :N_STATE])
    xim[...] = _dot(ub, bblk_ref[:, N_STATE:])

    for c in range(N_STATE // S5_LANE_CHUNK):
        ls = slice(c * S5_LANE_CHUNK, (c + 1) * S5_LANE_CHUNK)

        def body(b, carry, ls=ls):
            cr, ci = carry
            r0 = pl.multiple_of(b * SUBLANES, SUBLANES)
            xr = xre[pl.ds(r0, SUBLANES), ls]
            xi = xim[pl.ds(r0, SUBLANES), ls]
            for k, s in enumerate((1, 2, 4)):
                pr = tabs_ref[2 * k, :, ls]
                pi = tabs_ref[2 * k + 1, :, ls]
                rr = pltpu.roll(xr, s, 0)
                ri = pltpu.roll(xi, s, 0)
                xr, xi = xr + (pr * rr - pi * ri), xi + (pr * ri + pi * rr)
            pr = tabs_ref[6, :, ls]
            pi = tabs_ref[7, :, ls]
            hr = xr + (pr * cr - pi * ci)
            hi = xi + (pr * ci + pi * cr)
            xre[pl.ds(r0, SUBLANES), ls] = hr
            xim[pl.ds(r0, SUBLANES), ls] = hi
            shape = (SUBLANES, S5_LANE_CHUNK)
            return (jnp.broadcast_to(hr[SUBLANES - 1:SUBLANES, :], shape),
                    jnp.broadcast_to(hi[SUBLANES - 1:SUBLANES, :], shape))

        cr, ci = lax.fori_loop(0, tt // SUBLANES, body, (cre[:, ls], cim[:, ls]))
        cre[:, ls] = cr
        cim[:, ls] = ci

    hre_ref[0] = cre[0:1, :]
    him_ref[0] = cim[0:1, :]
    y_lin = (_dot(xre[...].astype(BF16), cblk_ref[:N_STATE, :])
             + _dot(xim[...].astype(BF16), cblk_ref[N_STATE:, :]))
    y_ref[...] = _s5_output(y_lin, u, dskip_ref, wglu_ref, bglu_ref)


def _s5_prompt(u2d, bblk, cblk, tabs, dskip, wglu_b, bglu, n_seq, t_len, tt):
    nt = t_len // tt
    full = lambda a: pl.BlockSpec(a.shape, lambda b, t: (0,) * a.ndim)
    return pl.pallas_call(
        functools.partial(_s5_prompt_kernel, tt=tt),
        grid=(n_seq, nt),
        in_specs=[pl.BlockSpec((tt, SSM_WIDTH), lambda b, t: (b * nt + t, 0)),
                  full(bblk), full(cblk), full(tabs), full(dskip), full(wglu_b), full(bglu)],
        out_specs=[pl.BlockSpec((tt, SSM_WIDTH), lambda b, t: (b * nt + t, 0)),
                   pl.BlockSpec((1, 1, N_STATE), lambda b, t: (b, 0, 0)),
                   pl.BlockSpec((1, 1, N_STATE), lambda b, t: (b, 0, 0))],
        out_shape=[jax.ShapeDtypeStruct((n_seq * t_len, SSM_WIDTH), F32),
                   jax.ShapeDtypeStruct((n_seq, 1, N_STATE), F32),
                   jax.ShapeDtypeStruct((n_seq, 1, N_STATE), F32)],
        scratch_shapes=[pltpu.VMEM((tt, N_STATE), F32), pltpu.VMEM((tt, N_STATE), F32),
                        pltpu.VMEM((SUBLANES, N_STATE), F32), pltpu.VMEM((SUBLANES, N_STATE), F32)],
        compiler_params=_cparams("arbitrary", "arbitrary"),
        name="s5_prompt",
    )(u2d, bblk, cblk, tabs, dskip, wglu_b, bglu)


def _s5_step_kernel(u_ref, h0re_ref, h0im_ref, are_ref, aim_ref, bblk_ref, cblk_ref,
                    dskip_ref, wglu_ref, bglu_ref, y_ref, hre_ref, him_ref):
    u = u_ref[...]
    ub = u.astype(BF16)
    ar = are_ref[...]
    ai = aim_ref[...]
    h0r = h0re_ref[...]
    h0i = h0im_ref[...]
    hr = _dot(ub, bblk_ref[:, :N_STATE]) + (ar * h0r - ai * h0i)
    hi = _dot(ub, bblk_ref[:, N_STATE:]) + (ar * h0i + ai * h0r)
    hre_ref[...] = hr
    him_ref[...] = hi
    y_lin = _dot(hr.astype(BF16), cblk_ref[:N_STATE, :]) + _dot(hi.astype(BF16), cblk_ref[N_STATE:, :])
    y_ref[...] = _s5_output(y_lin, u, dskip_ref, wglu_ref, bglu_ref)


def _s5_step(u2d, h0re, h0im, are, aim, bblk, cblk, dskip, wglu_b, bglu):
    m = u2d.shape[0]
    args = (u2d, h0re, h0im, are, aim, bblk, cblk, dskip, wglu_b, bglu)
    full = lambda a: pl.BlockSpec(a.shape, lambda i: (0,) * a.ndim)
    return pl.pallas_call(
        _s5_step_kernel,
        grid=(1,),
        in_specs=[full(a) for a in args],
        out_specs=[pl.BlockSpec((m, SSM_WIDTH), lambda i: (0, 0)),
                   pl.BlockSpec((m, N_STATE), lambda i: (0, 0)),
                   pl.BlockSpec((m, N_STATE), lambda i: (0, 0))],
        out_shape=[jax.ShapeDtypeStruct((m, SSM_WIDTH), F32),
                   jax.ShapeDtypeStruct((m, N_STATE), F32),
                   jax.ShapeDtypeStruct((m, N_STATE), F32)],
        compiler_params=_cparams("arbitrary"),
        name="s5_step",
    )(*args)


def _topk_bias_t(gate_t, nblk):
    bidx = lax.broadcasted_iota(jnp.int32, gate_t.shape, 0).astype(F32)
    bias = jnp.full(gate_t.shape, NEG, F32)
    g = gate_t
    for _ in range(MOBA_TOPK):
        mx = jnp.max(g, axis=0, keepdims=True)
        am = jnp.min(jnp.where(g == mx, bidx, float(nblk)), axis=0, keepdims=True)
        hit = bidx == am
        bias = jnp.where(hit & (mx > -jnp.inf), 0.0, bias)
        g = jnp.where(hit, -jnp.inf, g)
    return bias


SCORE_LOOKAHEAD = 3


def _moba_prompt_kernel(q_ref, kb_ref, vta_ref, kmean_ref, slope_ref, o_ref,
                        qt_sc, sel_sc, m_sc, acc_sc, alibi_sc, *, nblk):
    i = pl.program_id(1)
    bq = MOBA_BLOCK
    key_idx = lax.broadcasted_iota(jnp.int32, (bq, bq), 0)
    qry_idx = lax.broadcasted_iota(jnp.int32, (bq, bq), 1)
    lane = lax.broadcasted_iota(jnp.int32, (1, LANES), 1)
    sub = lax.broadcasted_iota(jnp.int32, (LANES, 1), 0)
    gblk = lax.broadcasted_iota(jnp.int32, (nblk, bq), 0)

    @pl.when((pl.program_id(0) == 0) & (i == 0))
    def _():
        for h in range(N_HEADS):
            alibi_sc[h] = (slope_ref[h] * LOG2E) * key_idx.astype(F32)

    for p in range(N_HEADS // 2):
        ls = slice(p * LANES, (p + 1) * LANES)
        qp = q_ref[0, :, ls]
        q_hi, q_lo = _split_bf16(qp)
        qt = qp.T * (HEAD_DIM ** -0.5 * LOG2E)
        km = kmean_ref[0, :, ls]
        for hh in range(2):
            h = 2 * p + hh
            qt_sc[h] = jnp.where((sub // HEAD_DIM) == hh, qt, 0.0).astype(BF16)
            km_hi, km_lo = _split_bf16(jnp.where((lane // HEAD_DIM) == hh, km, 0.0))
            gate_t = _dot_nt(km_hi, q_hi) + _dot_nt(km_hi, q_lo) + _dot_nt(km_lo, q_hi)
            sel_sc[h] = _topk_bias_t(jnp.where(gblk < i, gate_t, -jnp.inf), nblk)

    def scores(j, h):
        c0 = pl.multiple_of(j * bq, bq)
        kj = kb_ref[0, pl.ds(c0, bq), (h // 2) * LANES:(h // 2 + 1) * LANES]
        return _dot(kj, qt_sc[h]) + alibi_sc[h]

    def values(j, h):
        return vta_ref[0, j, h * V_ROWS:(h + 1) * V_ROWS, :]

    def over_heads(score_fn, update_fn):
        pending = [score_fn(h) for h in range(SCORE_LOOKAHEAD)]
        for h in range(N_HEADS):
            if h + SCORE_LOOKAHEAD < N_HEADS:
                pending.append(score_fn(h + SCORE_LOOKAHEAD))
            update_fn(h, pending.pop(0))

    def own_update(h, s_t):
        s_t = jnp.where(key_idx <= qry_idx, s_t, NEG)
        m = jnp.max(s_t, axis=0, keepdims=True)
        m_sc[h] = jnp.broadcast_to(m, (SUBLANES, bq))
        acc_sc[h] = _dot(values(i, h), jnp.exp2(s_t - m).astype(BF16))

    over_heads(lambda h: scores(i, h), own_update)

    def body(j, carry):
        shift = ((j - i) * bq).astype(F32)

        def past_update(h, s_t):
            bias = sel_sc[h, pl.ds(j, 1), :] + (slope_ref[h] * LOG2E) * shift
            m_old = m_sc[h][0:1, :]
            m_new = jnp.maximum(m_old, jnp.max(s_t, axis=0, keepdims=True) + bias)
            m_sc[h] = jnp.broadcast_to(m_new, (SUBLANES, bq))
            pv = _dot(values(j, h), jnp.exp2(s_t - (m_new - bias)).astype(BF16))
            acc_sc[h] = jnp.exp2(m_old - m_new) * acc_sc[h] + pv

        over_heads(lambda h: scores(j, h), past_update)
        return carry

    lax.fori_loop(0, i, body, 0)

    for p in range(N_HEADS // 2):
        halves = []
        for h in (2 * p, 2 * p + 1):
            acc = acc_sc[h]
            halves.append(acc[:HEAD_DIM, :] / acc[HEAD_DIM:HEAD_DIM + 1, :])
        o_ref[0, :, p * LANES:(p + 1) * LANES] = jnp.concatenate(halves, axis=0).T


def _moba_prompt(q3, kb3, vta, kmean3, slopes):
    n, t_len, _ = q3.shape
    nblk = t_len // MOBA_BLOCK
    return pl.pallas_call(
        functools.partial(_moba_prompt_kernel, nblk=nblk),
        grid=(n, nblk),
        in_specs=[pl.BlockSpec((1, MOBA_BLOCK, ATT_WIDTH), lambda b, i: (b, i, 0)),
                  pl.BlockSpec((1, t_len, ATT_WIDTH), lambda b, i: (b, 0, 0)),
                  pl.BlockSpec((1, nblk, N_HEADS * V_ROWS, MOBA_BLOCK), lambda b, i: (b, 0, 0, 0)),
                  pl.BlockSpec((1, nblk, ATT_WIDTH), lambda b, i: (b, 0, 0)),
                  pl.BlockSpec(memory_space=pltpu.SMEM)],
        out_specs=pl.BlockSpec((1, MOBA_BLOCK, ATT_WIDTH), lambda b, i: (b, i, 0)),
        out_shape=jax.ShapeDtypeStruct((n, t_len, ATT_WIDTH), F32),
        scratch_shapes=[pltpu.VMEM((N_HEADS, LANES, MOBA_BLOCK), BF16),
                        pltpu.VMEM((N_HEADS, nblk, MOBA_BLOCK), F32),
                        pltpu.VMEM((N_HEADS, SUBLANES, MOBA_BLOCK), F32),
                        pltpu.VMEM((N_HEADS, V_ROWS, MOBA_BLOCK), F32),
                        pltpu.VMEM((N_HEADS, MOBA_BLOCK, MOBA_BLOCK), F32)],
        compiler_params=_cparams("arbitrary", "arbitrary"),
        name="moba_prompt",
    )(q3, kb3, vta, kmean3, slopes)


PAGES_PER_STEP = 32
PAGES_PER_BLOCK = MOBA_BLOCK // PAGE_SIZE
BLOCKS_PER_STEP = PAGES_PER_STEP // PAGES_PER_BLOCK


def _sample_scan_kernel(pt_ref, *refs, n_past):
    pages = refs[:PAGES_PER_STEP]
    qt_ref, knt_ref, s_ref, idx_ref, qrep_sc, g_sc = refs[PAGES_PER_STEP:]
    n = pl.program_id(0)
    c = pl.program_id(1)
    lane = lax.broadcasted_iota(jnp.int32, (1, LANES), 1)

    @pl.when(c == 0)
    def _():
        qcol = jnp.sum(jnp.where(lane == n, qt_ref[...], 0.0), axis=-1, keepdims=True)
        qrep_sc[...] = jnp.broadcast_to(qcol, qrep_sc.shape)
        g_sc[...] = jnp.zeros_like(g_sc)

    qrep = qrep_sc[...].reshape(N_HEADS, HEAD_DIM, LANES)
    g = g_sc[...]
    for r in range(0, PAGES_PER_STEP, PAGES_PER_BLOCK):
        blk = jnp.zeros((N_HEADS, LANES), F32)
        for rr in range(r, r + PAGES_PER_BLOCK):
            sc = jnp.sum(pages[rr][0] * qrep, axis=1)
            s_ref[0, rr] = sc
            blk = blk + sc
        g = jnp.where(lane == c * BLOCKS_PER_STEP + r // PAGES_PER_BLOCK,
                      jnp.sum(blk, axis=-1, keepdims=True), g)
    g_sc[...] = g

    @pl.when(c == pl.num_programs(1) - 1)
    def _():
        kcol = jnp.sum(jnp.where(lane == n, knt_ref[...], 0.0), axis=-1, keepdims=True)
        g_own = jnp.sum((qrep_sc[:, 0:1] * kcol).reshape(N_HEADS, HEAD_DIM, 1), axis=1)
        gate = jnp.where(lane == n_past, g_own, g) * (1.0 / MOBA_BLOCK)
        gsel = jnp.where(lane < n_past, gate, -jnp.inf)
        lanef = lane.astype(F32)
        for k in range(MOBA_TOPK):
            mx = jnp.max(gsel, axis=-1, keepdims=True)
            am = jnp.min(jnp.where(gsel == mx, lanef, float(LANES)), axis=-1, keepdims=True)
            idx_ref[0, k] = jnp.broadcast_to(am.astype(jnp.int32), (N_HEADS, LANES))
            gsel = jnp.where(lanef == am, -jnp.inf, gsel)


def _sample_scan(page_table, ck, q_t, knew_t):
    n, n_pages = page_table.shape
    steps = n_pages // PAGES_PER_STEP
    n_past = n_pages // PAGES_PER_BLOCK

    def page_spec(r):
        return pl.BlockSpec((1, N_HEADS, HEAD_DIM, PAGE_SIZE),
                            lambda b, c, pt: (pt[b * n_pages + c * PAGES_PER_STEP + r], 0, 0, 0))

    full = lambda a: pl.BlockSpec(a.shape, lambda b, c, pt: (0,) * a.ndim)
    return pl.pallas_call(
        functools.partial(_sample_scan_kernel, n_past=n_past),
        grid_spec=pltpu.PrefetchScalarGridSpec(
            num_scalar_prefetch=1,
            grid=(n, steps),
            in_specs=[page_spec(r) for r in range(PAGES_PER_STEP)] + [full(q_t), full(knew_t)],
            out_specs=[pl.BlockSpec((1, PAGES_PER_STEP, N_HEADS, PAGE_SIZE), lambda b, c, pt: (b, c, 0, 0)),
                       pl.BlockSpec((1, MOBA_TOPK, N_HEADS, LANES), lambda b, c, pt: (b, 0, 0, 0))],
            scratch_shapes=[pltpu.VMEM((ATT_WIDTH, LANES), F32), pltpu.VMEM((N_HEADS, LANES), F32)],
        ),
        out_shape=[jax.ShapeDtypeStruct((n, n_pages, N_HEADS, PAGE_SIZE), F32),
                   jax.ShapeDtypeStruct((n, MOBA_TOPK, N_HEADS, LANES), jnp.int32)],
        compiler_params=_cparams("arbitrary", "arbitrary"),
        name="sample_scan",
    )(page_table.reshape(-1), *([ck] * PAGES_PER_STEP), q_t, knew_t)


SEL_PAGES = MOBA_TOPK * PAGES_PER_BLOCK


def _sample_attend_kernel(pt_ref, ix_ref, slope_ref, s_ref, q_ref, knew_ref, vnew_ref, *refs, past_len):
    vt_refs, o_ref = refs[:SEL_PAGES], refs[SEL_PAGES]
    n = pl.program_id(0)
    h = pl.program_id(1)
    scale = HEAD_DIM ** -0.5
    slope = slope_ref[h]
    off = lax.broadcasted_iota(jnp.int32, (1, PAGE_SIZE), 1).astype(F32)
    qh = q_ref[0, pl.ds(h, 1), :]
    s0 = jnp.sum(qh * knew_ref[0, pl.ds(h, 1), :], axis=-1, keepdims=True) * scale
    rows = []
    for s in range(MOBA_TOPK):
        blk = ix_ref[(n * MOBA_TOPK + s) * N_HEADS + h]
        for half in range(PAGES_PER_BLOCK):
            raw = s_ref[0, blk * PAGES_PER_BLOCK + half, pl.ds(h, 1), :]
            dist = (past_len - blk * MOBA_BLOCK - half * PAGE_SIZE).astype(F32) - off
            rows.append(raw * scale - slope * dist)
    sc = jnp.concatenate(rows, axis=1)
    m = jnp.maximum(s0, jnp.max(sc, axis=-1, keepdims=True))
    p = jnp.exp(sc - m)
    p0 = jnp.exp(s0 - m)
    l = p0 + jnp.sum(p, axis=-1, keepdims=True)
    vt = jnp.concatenate([vt_refs[w][0, h] for w in range(SEL_PAGES)], axis=1)
    pv = _dot_nt(jnp.broadcast_to(p, (SUBLANES, p.shape[1])).astype(BF16), vt.astype(BF16))[0:1]
    o_ref[0, pl.ds(h, 1), :] = (p0 * vnew_ref[0, pl.ds(h, 1), :] + pv) / l


def _sample_attend(page_table, idx, slopes, s_all, q3, knew3, vnew3, cv, past_len):
    n = q3.shape[0]
    n_pages = page_table.shape[1]

    def vt_spec(w):
        def index(b, h, pt, ix, sl):
            blk = ix[(b * MOBA_TOPK + w // PAGES_PER_BLOCK) * N_HEADS + h]
            return (pt[b * n_pages + blk * PAGES_PER_BLOCK + w % PAGES_PER_BLOCK], 0, 0, 0)
        return pl.BlockSpec((1, N_HEADS, HEAD_DIM, PAGE_SIZE), index)

    tok = pl.BlockSpec((1, N_HEADS, HEAD_DIM), lambda b, h, pt, ix, sl: (b, 0, 0))
    return pl.pallas_call(
        functools.partial(_sample_attend_kernel, past_len=past_len),
        grid_spec=pltpu.PrefetchScalarGridSpec(
            num_scalar_prefetch=3,
            grid=(n, N_HEADS),
            in_specs=[pl.BlockSpec((1,) + s_all.shape[1:], lambda b, h, pt, ix, sl: (b, 0, 0, 0)), tok, tok, tok]
                     + [vt_spec(w) for w in range(SEL_PAGES)],
            out_specs=tok,
        ),
        out_shape=jax.ShapeDtypeStruct((n, N_HEADS, HEAD_DIM), F32),
        compiler_params=_cparams("arbitrary", "arbitrary"),
        name="sample_attend",
    )(page_table.reshape(-1), idx.reshape(-1), slopes, s_all, q3, knew3, vnew3, *([cv] * SEL_PAGES))


def _merge_kernel(x_ref, ys_ref, ya_ref, gs_ref, ga_ref, wa_ref, wb_ref, wo_ref, o_ref):
    a = _dot(ys_ref[...].astype(BF16), wa_ref[...])
    b = _dot(ya_ref[...].astype(BF16), wb_ref[...])
    mix = jax.nn.sigmoid(gs_ref[...]) * a + jax.nn.sigmoid(ga_ref[...]) * b
    o_ref[...] = x_ref[...] + _dot(mix.astype(BF16), wo_ref[...])


def _merge(x2d, ys, ya, gs, ga, wa_b, wb_b, wo_b, tm):
    m = x2d.shape[0]
    row = lambda w: pl.BlockSpec((tm, w), lambda i: (i, 0))
    full = lambda a: pl.BlockSpec(a.shape, lambda i: (0,) * a.ndim)
    return pl.pallas_call(
        _merge_kernel,
        grid=(m // tm,),
        in_specs=[row(D_MODEL), row(SSM_WIDTH), row(ATT_WIDTH), row(D_MODEL), row(D_MODEL),
                  full(wa_b), full(wb_b), full(wo_b)],
        out_specs=row(D_MODEL),
        out_shape=jax.ShapeDtypeStruct((m, D_MODEL), F32),
        compiler_params=_cparams("arbitrary"),
        name="merge",
    )(x2d, ys, ya, gs, ga, wa_b, wb_b, wo_b)


FF_CHUNK = D_FF // 2


def _ffn_kernel(x_ref, *refs, tm, sequential):
    if sequential:
        (n2_ref, wup_ref, wgate_ref, cw_ref, cb_ref, wdown_ref, o_ref, cs_ref, tail) = refs
    else:
        (buf0_ref, buf1_ref, n2_ref, wup_ref, wgate_ref, cw_ref, cb_ref, wdown_ref, o_ref, cs0_ref, cs1_ref) = refs
    x = x_ref[...]
    ms = jnp.mean(x * x, axis=-1, keepdims=True)
    h = (x * lax.rsqrt(ms + NORM_EPS) * n2_ref[...]).astype(BF16)

    if sequential:
        @pl.when(pl.program_id(1) == 0)
        def _():
            tail[...] = jnp.zeros_like(tail)
        rows = lax.broadcasted_iota(jnp.int32, (tm, 1), 0)

    y = x
    for c in range(D_FF // FF_CHUNK):
        cs = slice(c * FF_CHUNK, (c + 1) * FF_CHUNK)
        up = _dot(h, wup_ref[:, cs])
        g = _dot(h, wgate_ref[:, cs])
        if sequential:
            p2 = tail[SUBLANES - 2:SUBLANES - 1, cs]
            p1 = tail[SUBLANES - 1:SUBLANES, cs]
            g1 = jnp.where(rows == 0, p1, pltpu.roll(g, 1, 0))
            g2 = jnp.where(rows == 0, p2, jnp.where(rows == 1, p1, pltpu.roll(g, 2, 0)))
            tail[:, cs] = g[tm - SUBLANES:, :]
            cs_ref[0, :, cs] = g[tm - 2:, :]
        else:
            g2 = buf0_ref[:, cs]
            g1 = buf1_ref[:, cs]
            cs0_ref[:, cs] = g1
            cs1_ref[:, cs] = g
        conv = cb_ref[:, cs] + cw_ref[0:1, cs] * g2 + cw_ref[1:2, cs] * g1 + cw_ref[2:3, cs] * g
        act = (jax.nn.gelu(conv) * up).astype(BF16)
        y = y + _dot(act, wdown_ref[cs, :])
    o_ref[...] = y


def _ffn_prompt(x2d, n2, wup_b, wgate_b, cw, cb, wdown_b, n_seq, t_len, tm):
    nt = t_len // tm
    full = lambda a: pl.BlockSpec(a.shape, lambda b, t: (0,) * a.ndim)
    row = pl.BlockSpec((tm, D_MODEL), lambda b, t: (b * nt + t, 0))
    return pl.pallas_call(
        functools.partial(_ffn_kernel, tm=tm, sequential=True),
        grid=(n_seq, nt),
        in_specs=[row, full(n2), full(wup_b), full(wgate_b), full(cw), full(cb), full(wdown_b)],
        out_specs=[row, pl.BlockSpec((1, 2, D_FF), lambda b, t: (b, 0, 0))],
        out_shape=[jax.ShapeDtypeStruct((n_seq * t_len, D_MODEL), F32),
                   jax.ShapeDtypeStruct((n_seq, 2, D_FF), F32)],
        scratch_shapes=[pltpu.VMEM((SUBLANES, D_FF), F32)],
        compiler_params=_cparams("arbitrary", "arbitrary"),
        name="ffn_prompt",
    )(x2d, n2, wup_b, wgate_b, cw, cb, wdown_b)


def _ffn_step(x2d, buf0, buf1, n2, wup_b, wgate_b, cw, cb, wdown_b):
    m = x2d.shape[0]
    args = (x2d, buf0, buf1, n2, wup_b, wgate_b, cw, cb, wdown_b)
    full = lambda a: pl.BlockSpec(a.shape, lambda i: (0,) * a.ndim)
    return pl.pallas_call(
        functools.partial(_ffn_kernel, tm=m, sequential=False),
        grid=(1,),
        in_specs=[full(a) for a in args],
        out_specs=[pl.BlockSpec((m, D_MODEL), lambda i: (0, 0)),
                   pl.BlockSpec((m, D_FF), lambda i: (0, 0)),
                   pl.BlockSpec((m, D_FF), lambda i: (0, 0))],
        out_shape=[jax.ShapeDtypeStruct((m, D_MODEL), F32),
                   jax.ShapeDtypeStruct((m, D_FF), F32),
                   jax.ShapeDtypeStruct((m, D_FF), F32)],
        compiler_params=_cparams("arbitrary"),
        name="ffn_step",
    )(*args)


def _s5_params(lam_re, lam_im, log_dt, b_re, b_im, c_re, c_im):
    lr, li = lam_re.astype(F32), lam_im.astype(F32)
    dt = jnp.exp(log_dt.astype(F32))[:, None]
    mag = jnp.exp(lr * dt)
    ar, ai = mag * jnp.cos(li * dt), mag * jnp.sin(li * dt)
    den = lr * lr + li * li
    fr = ((ar - 1.0) * lr + ai * li) / den
    fi = (ai * lr - (ar - 1.0) * li) / den
    br_, bi_ = b_re.astype(F32), b_im.astype(F32)
    bbr = fr[..., None] * br_ - fi[..., None] * bi_
    bbi = fr[..., None] * bi_ + fi[..., None] * br_
    eye = jnp.eye(SSM_GROUPS, dtype=F32)

    def bdiag_in(w):
        return jnp.einsum('gsc,gh->gchs', w, eye).reshape(SSM_WIDTH, N_STATE)

    def bdiag_out(w):
        return jnp.einsum('gcs,gh->gshc', w, eye).reshape(N_STATE, SSM_WIDTH)

    bblk = jnp.concatenate([bdiag_in(bbr), bdiag_in(bbi)], axis=1).astype(BF16)
    cblk = jnp.concatenate([bdiag_out(c_re.astype(F32)), -bdiag_out(c_im.astype(F32))], axis=0).astype(BF16)

    a_re, a_im = ar.reshape(1, N_STATE), ai.reshape(1, N_STATE)

    def cmul(x, y):
        return x[0] * y[0] - x[1] * y[1], x[0] * y[1] + x[1] * y[0]

    pows = [(a_re, a_im)]
    for _ in range(SUBLANES - 1):
        pows.append(cmul(pows[-1], (a_re, a_im)))
    rows = jnp.arange(SUBLANES)[:, None]
    tabs = []
    for s in (1, 2, 4):
        tabs.append(jnp.where(rows >= s, pows[s - 1][0], 0.0))
        tabs.append(jnp.where(rows >= s, pows[s - 1][1], 0.0))
    tabs.append(jnp.concatenate([p[0] for p in pows], axis=0))
    tabs.append(jnp.concatenate([p[1] for p in pows], axis=0))
    return a_re, a_im, bblk, cblk, jnp.stack(tabs)


def kernel(x_prompt, x_sample, cache_k, cache_v, state_ssm_re, state_ssm_im, state_conv, page_table,
           norm1_w, w_in, q_norm_w, k_norm_w, lam_re, lam_im, log_dt, b_re, b_im, c_re, c_im, d_skip,
           w_glu, b_glu, w_a, w_b, w_o, norm2_w, w_up, w_gate, conv_w, conv_b, w_down):
    depth = norm1_w.shape[0]
    batch, seq, _ = x_prompt.shape
    dec_batch, dec_seq, _ = x_sample.shape
    assert depth == 1 and dec_seq == 1 and seq % MOBA_BLOCK == 0 and dec_batch <= LANES
    past_len = page_table.shape[1] * PAGE_SIZE
    slopes = 2.0 ** (-8.0 * jnp.arange(1, N_HEADS + 1, dtype=F32) / N_HEADS)
    seg = jnp.kron(jnp.eye(N_HEADS, dtype=F32), jnp.ones((HEAD_DIM, HEAD_DIM), F32)).astype(BF16)

    l = 0
    row = lambda a: a.reshape(1, -1).astype(F32)
    n1, n2 = row(norm1_w[l]), row(norm2_w[l])
    qn_row = row(jnp.tile(q_norm_w[l], N_HEADS))
    kn_row = row(jnp.tile(k_norm_w[l], N_HEADS))
    win_b = w_in[l].astype(BF16)
    a_re, a_im, bblk, cblk, tabs = _s5_params(lam_re[l], lam_im[l], log_dt[l], b_re[l], b_im[l], c_re[l], c_im[l])
    dskip, bglu = row(d_skip[l]), row(b_glu[l])
    wglu_b = w_glu[l].astype(BF16)
    wa_b, wb_b, wo_b = w_a[l].astype(BF16), w_b[l].astype(BF16), w_o[l].astype(BF16)
    wup_b, wgate_b, wdown_b = w_up[l].astype(BF16), w_gate[l].astype(BF16), w_down[l].astype(BF16)
    cw, cb = conv_w[l].astype(F32), row(conv_b[l])

    mp = batch * seq
    xp = x_prompt.reshape(mp, D_MODEL)
    u, q, gs, ga, k_t, v_t, kb, vta, kmean = _inproj(xp, n1, win_b, qn_row, kn_row, seg, batch, seq, 256, True)
    y_ssm, hre_p, him_p = _s5_prompt(u, bblk, cblk, tabs, dskip, wglu_b, bglu, batch, seq, 256)
    y_att = _moba_prompt(q.reshape(batch, seq, ATT_WIDTH), kb.reshape(batch, seq, ATT_WIDTH), vta,
                         kmean.reshape(batch, seq // MOBA_BLOCK, ATT_WIDTH), slopes)
    x1 = _merge(xp, y_ssm, y_att.reshape(mp, ATT_WIDTH), gs, ga, wa_b, wb_b, wo_b, 512)
    yp, conv_p = _ffn_prompt(x1, n2, wup_b, wgate_b, cw, cb, wdown_b, batch, seq, 256)

    xs = x_sample.reshape(dec_batch, D_MODEL)
    u_s, q_s, gs_s, ga_s, k_s, v_s = _inproj(xs, n1, win_b, qn_row, kn_row, seg, 1, dec_batch, dec_batch, False)
    y_ssm_s, hre_s, him_s = _s5_step(u_s, state_ssm_re[l].reshape(dec_batch, N_STATE),
                                     state_ssm_im[l].reshape(dec_batch, N_STATE),
                                     a_re, a_im, bblk, cblk, dskip, wglu_b, bglu)
    heads = lambda a: a.reshape(dec_batch, N_HEADS, HEAD_DIM)
    pad = lambda a_t: jnp.pad(a_t, ((0, 0), (0, LANES - dec_batch)))
    ck = jnp.transpose(cache_k[l], (0, 2, 3, 1))
    cv = jnp.transpose(cache_v[l], (0, 2, 3, 1))
    s_all, idx = _sample_scan(page_table, ck, pad(q_s.T), pad(k_s.T))
    y_att_s = _sample_attend(page_table, idx[..., 0], slopes, s_all, heads(q_s), heads(k_s), heads(v_s), cv, past_len)
    x1_s = _merge(xs, y_ssm_s, y_att_s.reshape(dec_batch, ATT_WIDTH), gs_s, ga_s, wa_b, wb_b, wo_b, dec_batch)
    ys, cs0, cs1 = _ffn_step(x1_s, state_conv[l, :, 0, :], state_conv[l, :, 1, :], n2, wup_b, wgate_b, cw, cb, wdown_b)

    g, s = SSM_GROUPS, SSM_STATE
    kv_out = lambda a_t: jnp.transpose(a_t.reshape(batch, N_HEADS, HEAD_DIM, seq), (0, 3, 1, 2))[None]
    return (yp.reshape(batch, seq, D_MODEL),
            ys.reshape(dec_batch, 1, D_MODEL),
            kv_out(k_t),
            kv_out(v_t),
            hre_p.reshape(1, batch, g, s),
            him_p.reshape(1, batch, g, s),
            conv_p.reshape(1, batch, 2, D_FF),
            k_s.reshape(1, dec_batch, 1, N_HEADS, HEAD_DIM),
            v_s.reshape(1, dec_batch, 1, N_HEADS, HEAD_DIM),
            hre_s.reshape(1, dec_batch, g, s),
            him_s.reshape(1, dec_batch, g, s),
            jnp.stack([cs0, cs1], axis=1).reshape(1, dec_batch, 2, D_FF))
```

```python
import functools

import jax
import jax.numpy as jnp
from jax import lax
from jax.experimental import pallas as pl
from jax.experimental.pallas import tpu as pltpu

F32 = jnp.float32
BF16 = jnp.bfloat16

D_MODEL = 1024
SSM_WIDTH = 512
SSM_GROUP = 16
SSM_GROUPS = 32
SSM_STATE = 64
N_STATE = SSM_GROUPS * SSM_STATE
N_HEADS = 8
HEAD_DIM = 64
ATT_WIDTH = 512
MOBA_BLOCK = 256
MOBA_TOPK = 3
D_FF = 2816
NORM_EPS = 1e-6
PAGE_SIZE = 128
LOG2E = 1.4426950408889634
NEG = -1e30

VMEM_LIMIT_BYTES = 56 * 1024 * 1024
SUBLANES = 8
LANES = 128


def _cparams(*sem):
    return pltpu.CompilerParams(dimension_semantics=sem, vmem_limit_bytes=VMEM_LIMIT_BYTES)


def _dot(a, b):
    return jnp.dot(a, b, preferred_element_type=F32)


def _dot_nt(a, b):
    return lax.dot_general(a, b, (((1,), (1,)), ((), ())), preferred_element_type=F32)


def _split_bf16(a):
    hi = a.astype(BF16)
    lo = (a - hi.astype(F32)).astype(BF16)
    return hi, lo


V_ROWS = 80


def _inproj_kernel(x_ref, n1_ref, win_ref, qn_ref, kn_ref, seg_ref, u_ref, q_ref, gs_ref, ga_ref, *rest,
                   tm, prompt):
    x = x_ref[...]
    ms = jnp.mean(x * x, axis=-1, keepdims=True)
    h = (x * lax.rsqrt(ms + NORM_EPS) * n1_ref[...]).astype(BF16)

    def proj(lo, hi):
        return _dot(h, win_ref[:, lo:hi])

    def head_rmsnorm(z, w_row):
        sq_hi, sq_lo = _split_bf16(z * z)
        ss = _dot(sq_hi, seg_ref[...]) + _dot(sq_lo, seg_ref[...])
        return z * lax.rsqrt(ss * (1.0 / HEAD_DIM) + NORM_EPS) * w_row

    o1 = SSM_WIDTH
    o2 = o1 + ATT_WIDTH
    o3 = o2 + ATT_WIDTH
    o4 = o3 + ATT_WIDTH
    o5 = o4 + D_MODEL
    u_ref[...] = proj(0, o1)
    q_ref[...] = head_rmsnorm(proj(o1, o2), qn_ref[...])
    k = head_rmsnorm(proj(o2, o3), kn_ref[...])
    v = proj(o3, o4)
    gs_ref[...] = proj(o4, o5)
    ga_ref[...] = proj(o5, o5 + D_MODEL)
    if not prompt:
        k_ref, v_ref = rest
        k_ref[...] = k
        v_ref[...] = v
        return
    kt_ref, vt_ref, kb_ref, vta_ref, kmean_ref = rest
    nb = tm // MOBA_BLOCK
    kt_ref[0] = k.T
    vt = v.T
    vt_ref[0] = vt
    kb_ref[...] = k.astype(BF16)
    ones = jnp.ones((V_ROWS - HEAD_DIM, MOBA_BLOCK), BF16)
    for hd in range(N_HEADS):
        for b in range(nb):
            vta_ref[0, b, hd * V_ROWS:hd * V_ROWS + HEAD_DIM, :] = (
                vt[hd * HEAD_DIM:(hd + 1) * HEAD_DIM, b * MOBA_BLOCK:(b + 1) * MOBA_BLOCK].astype(BF16))
            vta_ref[0, b, hd * V_ROWS + HEAD_DIM:(hd + 1) * V_ROWS, :] = ones
    ksum = k.reshape(nb, MOBA_BLOCK, ATT_WIDTH).sum(axis=1)
    kmean_ref[...] = (ksum * (1.0 / MOBA_BLOCK)).reshape(nb, 1, ATT_WIDTH)


def _inproj(x2d, n1, win_b, qn_row, kn_row, seg, n_seq, t_len, tm, prompt):
    nt = t_len // tm
    m = n_seq * t_len
    row = lambda w: pl.BlockSpec((tm, w), lambda b, t: (b * nt + t, 0))
    full = lambda a: pl.BlockSpec(a.shape, lambda b, t: (0,) * a.ndim)
    f32 = lambda *s: jax.ShapeDtypeStruct(s, F32)
    out_shape = [f32(m, SSM_WIDTH), f32(m, ATT_WIDTH), f32(m, D_MODEL), f32(m, D_MODEL)]
    out_specs = [row(SSM_WIDTH), row(ATT_WIDTH), row(D_MODEL), row(D_MODEL)]
    if prompt:
        nb = tm // MOBA_BLOCK
        tspec = pl.BlockSpec((1, ATT_WIDTH, tm), lambda b, t: (b, 0, t))
        out_shape += [f32(n_seq, ATT_WIDTH, t_len), f32(n_seq, ATT_WIDTH, t_len),
                      jax.ShapeDtypeStruct((m, ATT_WIDTH), BF16),
                      jax.ShapeDtypeStruct((n_seq, t_len // MOBA_BLOCK, N_HEADS * V_ROWS, MOBA_BLOCK), BF16),
                      f32(m // MOBA_BLOCK, 1, ATT_WIDTH)]
        out_specs += [tspec, tspec, row(ATT_WIDTH),
                      pl.BlockSpec((1, nb, N_HEADS * V_ROWS, MOBA_BLOCK), lambda b, t: (b, t, 0, 0)),
                      pl.BlockSpec((nb, 1, ATT_WIDTH), lambda b, t: (b * nt + t, 0, 0))]
    else:
        out_shape += [f32(m, ATT_WIDTH), f32(m, ATT_WIDTH)]
        out_specs += [row(ATT_WIDTH), row(ATT_WIDTH)]
    return pl.pallas_call(
        functools.partial(_inproj_kernel, tm=tm, prompt=prompt),
        grid=(n_seq, nt),
        in_specs=[row(D_MODEL), full(n1), full(win_b), full(qn_row), full(kn_row), full(seg)],
        out_specs=out_specs,
        out_shape=out_shape,
        compiler_params=_cparams("arbitrary", "arbitrary"),
        name="inproj",
    )(x2d, n1, win_b, qn_row, kn_row, seg)


def _s5_output(y_lin, u, dskip_ref, wglu_ref, bglu_ref):
    y = jax.nn.gelu(y_lin + dskip_ref[...] * u)
    gate = jax.nn.sigmoid(_dot(y.astype(BF16), wglu_ref[...]) + bglu_ref[...])
    return y * gate


S5_LANE_CHUNK = 512


def _s5_prompt_kernel(u_ref, bb_ref, cc_ref, tabs_ref, dskip_ref, wglu_ref, bglu_ref,
                      y_ref, hre_ref, him_ref, xre, xim, cre, cim, *, tt):
    t = pl.program_id(1)

    @pl.when(t == 0)
    def _():
        cre[...] = jnp.zeros_like(cre)
        cim[...] = jnp.zeros_like(cim)

    u = u_ref[...]
    ub = u.astype(BF16)
    n_in = S5_LANE_CHUNK // SSM_STATE * SSM_GROUP
    y_parts = []
    for c in range(N_STATE // S5_LANE_CHUNK):
        ls = slice(c * S5_LANE_CHUNK, (c + 1) * S5_LANE_CHUNK)
        us = ub[:, c * n_in:(c + 1) * n_in]
        xre[:, ls] = _dot(us, bb_ref[c, :, :S5_LANE_CHUNK])
        xim[:, ls] = _dot(us, bb_ref[c, :, S5_LANE_CHUNK:])
        cr, ci = cre[:, ls], cim[:, ls]
        for b in range(tt // SUBLANES):
            rs = slice(b * SUBLANES, (b + 1) * SUBLANES)
            xr = xre[rs, ls]
            xi = xim[rs, ls]
            for k, s in enumerate((1, 2, 4)):
                pr = tabs_ref[2 * k, :, ls]
                pi = tabs_ref[2 * k + 1, :, ls]
                rr = pltpu.roll(xr, s, 0)
                ri = pltpu.roll(xi, s, 0)
                xr, xi = xr + (pr * rr - pi * ri), xi + (pr * ri + pi * rr)
            pr = tabs_ref[6, :, ls]
            pi = tabs_ref[7, :, ls]
            hr = xr + (pr * cr - pi * ci)
            hi = xi + (pr * ci + pi * cr)
            xre[rs, ls] = hr
            xim[rs, ls] = hi
            shape = (SUBLANES, S5_LANE_CHUNK)
            cr = jnp.broadcast_to(hr[SUBLANES - 1:SUBLANES, :], shape)
            ci = jnp.broadcast_to(hi[SUBLANES - 1:SUBLANES, :], shape)
        cre[:, ls] = cr
        cim[:, ls] = ci
        y_parts.append(_dot(xre[:, ls].astype(BF16), cc_ref[c, :S5_LANE_CHUNK, :])
                       + _dot(xim[:, ls].astype(BF16), cc_ref[c, S5_LANE_CHUNK:, :]))

    hre_ref[0] = cre[0:1, :]
    him_ref[0] = cim[0:1, :]
    y_ref[...] = _s5_output(jnp.concatenate(y_parts, axis=1), u, dskip_ref, wglu_ref, bglu_ref)


def _s5_prompt(u2d, bblk, cblk, tabs, dskip, wglu_b, bglu, n_seq, t_len, tt):
    nt = t_len // tt
    full = lambda a: pl.BlockSpec(a.shape, lambda b, t: (0,) * a.ndim)
    return pl.pallas_call(
        functools.partial(_s5_prompt_kernel, tt=tt),
        grid=(n_seq, nt),
        in_specs=[pl.BlockSpec((tt, SSM_WIDTH), lambda b, t: (b * nt + t, 0)),
                  full(bblk), full(cblk), full(tabs), full(dskip), full(wglu_b), full(bglu)],
        out_specs=[pl.BlockSpec((tt, SSM_WIDTH), lambda b, t: (b * nt + t, 0)),
                   pl.BlockSpec((1, 1, N_STATE), lambda b, t: (b, 0, 0)),
                   pl.BlockSpec((1, 1, N_STATE), lambda b, t: (b, 0, 0))],
        out_shape=[jax.ShapeDtypeStruct((n_seq * t_len, SSM_WIDTH), F32),
                   jax.ShapeDtypeStruct((n_seq, 1, N_STATE), F32),
                   jax.ShapeDtypeStruct((n_seq, 1, N_STATE), F32)],
        scratch_shapes=[pltpu.VMEM((tt, N_STATE), F32), pltpu.VMEM((tt, N_STATE), F32),
                        pltpu.VMEM((SUBLANES, N_STATE), F32), pltpu.VMEM((SUBLANES, N_STATE), F32)],
        compiler_params=_cparams("arbitrary", "arbitrary"),
        name="s5_prompt",
    )(u2d, bblk, cblk, tabs, dskip, wglu_b, bglu)


def _s5_step_kernel(u_ref, h0re_ref, h0im_ref, are_ref, aim_ref, bblk_ref, cblk_ref,
                    dskip_ref, wglu_ref, bglu_ref, y_ref, hre_ref, him_ref):
    u = u_ref[...]
    ub = u.astype(BF16)
    ar = are_ref[...]
    ai = aim_ref[...]
    h0r = h0re_ref[...]
    h0i = h0im_ref[...]
    hr = _dot(ub, bblk_ref[:, :N_STATE]) + (ar * h0r - ai * h0i)
    hi = _dot(ub, bblk_ref[:, N_STATE:]) + (ar * h0i + ai * h0r)
    hre_ref[...] = hr
    him_ref[...] = hi
    y_lin = _dot(hr.astype(BF16), cblk_ref[:N_STATE, :]) + _dot(hi.astype(BF16), cblk_ref[N_STATE:, :])
    y_ref[...] = _s5_output(y_lin, u, dskip_ref, wglu_ref, bglu_ref)


def _s5_step(u2d, h0re, h0im, are, aim, bblk, cblk, dskip, wglu_b, bglu):
    m = u2d.shape[0]
    args = (u2d, h0re, h0im, are, aim, bblk, cblk, dskip, wglu_b, bglu)
    full = lambda a: pl.BlockSpec(a.shape, lambda i: (0,) * a.ndim)
    return pl.pallas_call(
        _s5_step_kernel,
        grid=(1,),
        in_specs=[full(a) for a in args],
        out_specs=[pl.BlockSpec((m, SSM_WIDTH), lambda i: (0, 0)),
                   pl.BlockSpec((m, N_STATE), lambda i: (0, 0)),
                   pl.BlockSpec((m, N_STATE), lambda i: (0, 0))],
        out_shape=[jax.ShapeDtypeStruct((m, SSM_WIDTH), F32),
                   jax.ShapeDtypeStruct((m, N_STATE), F32),
                   jax.ShapeDtypeStruct((m, N_STATE), F32)],
        compiler_params=_cparams("arbitrary"),
        name="s5_step",
    )(*args)


def _topk_bias_t(gate_t, nblk):
    bidx = lax.broadcasted_iota(jnp.int32, gate_t.shape, 0).astype(F32)
    bias = jnp.full(gate_t.shape, NEG, F32)
    g = gate_t
    for _ in range(MOBA_TOPK):
        mx = jnp.max(g, axis=0, keepdims=True)
        am = jnp.min(jnp.where(g == mx, bidx, float(nblk)), axis=0, keepdims=True)
        hit = bidx == am
        bias = jnp.where(hit & (mx > -jnp.inf), 0.0, bias)
        g = jnp.where(hit, -jnp.inf, g)
    return bias


SCORE_LOOKAHEAD = 3
PAST_BLOCKS_PER_TRIP = 4


def _moba_prompt_kernel(q_ref, kb_ref, vta_ref, kmean_ref, slope_ref, o_ref,
                        qt_sc, sel_sc, m_sc, acc_sc, alibi_sc, *, nblk):
    i = pl.program_id(1)
    bq = MOBA_BLOCK
    key_idx = lax.broadcasted_iota(jnp.int32, (bq, bq), 0)
    qry_idx = lax.broadcasted_iota(jnp.int32, (bq, bq), 1)
    lane = lax.broadcasted_iota(jnp.int32, (1, LANES), 1)
    sub = lax.broadcasted_iota(jnp.int32, (LANES, 1), 0)
    gblk = lax.broadcasted_iota(jnp.int32, (nblk, bq), 0)

    @pl.when((pl.program_id(0) == 0) & (i == 0))
    def _():
        for h in range(N_HEADS):
            alibi_sc[h] = (slope_ref[h] * LOG2E) * key_idx.astype(F32)

    for p in range(N_HEADS // 2):
        ls = slice(p * LANES, (p + 1) * LANES)
        qp = q_ref[0, :, ls]
        q_hi, q_lo = _split_bf16(qp)
        qt = qp.T * (HEAD_DIM ** -0.5 * LOG2E)
        km = kmean_ref[0, :, ls]
        for hh in range(2):
            h = 2 * p + hh
            qt_sc[h] = jnp.where((sub // HEAD_DIM) == hh, qt, 0.0).astype(BF16)
            km_hi, km_lo = _split_bf16(jnp.where((lane // HEAD_DIM) == hh, km, 0.0))
            gate_t = _dot_nt(km_hi, q_hi) + _dot_nt(km_hi, q_lo) + _dot_nt(km_lo, q_hi)
            sel_sc[h] = _topk_bias_t(jnp.where(gblk < i, gate_t, -jnp.inf), nblk)

    def scores(j, h):
        c0 = pl.multiple_of(j * bq, bq)
        kj = kb_ref[0, pl.ds(c0, bq), (h // 2) * LANES:(h // 2 + 1) * LANES]
        return _dot(kj, qt_sc[h]) + alibi_sc[h]

    def values(j, h):
        return vta_ref[0, j, h * V_ROWS:(h + 1) * V_ROWS, :]

    def pipelined(blocks, update_fn):
        tasks = [(j, h) for j in blocks for h in range(N_HEADS)]
        pending = [scores(*t) for t in tasks[:SCORE_LOOKAHEAD]]
        for k, (j, h) in enumerate(tasks):
            if k + SCORE_LOOKAHEAD < len(tasks):
                pending.append(scores(*tasks[k + SCORE_LOOKAHEAD]))
            update_fn(j, h, pending.pop(0))

    def own_update(j, h, s_t):
        s_t = jnp.where(key_idx <= qry_idx, s_t, NEG)
        m = jnp.max(s_t, axis=0, keepdims=True)
        m_sc[h] = jnp.broadcast_to(m, (SUBLANES, bq))
        acc_sc[h] = _dot(values(j, h), jnp.exp2(s_t - m).astype(BF16))

    pipelined([i], own_update)

    def past_update(j, h, s_t):
        bias = sel_sc[h, pl.ds(j, 1), :] + (slope_ref[h] * LOG2E) * ((j - i) * bq).astype(F32)
        m_old = m_sc[h][0:1, :]
        m_new = jnp.maximum(m_old, jnp.max(s_t, axis=0, keepdims=True) + bias)
        m_sc[h] = jnp.broadcast_to(m_new, (SUBLANES, bq))
        pv = _dot(values(j, h), jnp.exp2(s_t - (m_new - bias)).astype(BF16))
        acc_sc[h] = jnp.exp2(m_old - m_new) * acc_sc[h] + pv

    def body(t, carry):
        pipelined([PAST_BLOCKS_PER_TRIP * t + d for d in range(PAST_BLOCKS_PER_TRIP)], past_update)
        return carry

    n_trips = i // PAST_BLOCKS_PER_TRIP
    lax.fori_loop(0, n_trips, body, 0)
    j_left = n_trips * PAST_BLOCKS_PER_TRIP
    width = PAST_BLOCKS_PER_TRIP // 2
    while width:
        @pl.when((i & width) != 0)
        def _(j_left=j_left, width=width):
            pipelined([j_left + d for d in range(width)], past_update)

        j_left = j_left + (i & width)
        width //= 2

    for p in range(N_HEADS // 2):
        halves = []
        for h in (2 * p, 2 * p + 1):
            acc = acc_sc[h]
            halves.append(acc[:HEAD_DIM, :] / acc[HEAD_DIM:HEAD_DIM + 1, :])
        o_ref[0, :, p * LANES:(p + 1) * LANES] = jnp.concatenate(halves, axis=0).T


def _moba_prompt(q3, kb3, vta, kmean3, slopes):
    n, t_len, _ = q3.shape
    nblk = t_len // MOBA_BLOCK
    return pl.pallas_call(
        functools.partial(_moba_prompt_kernel, nblk=nblk),
        grid=(n, nblk),
        in_specs=[pl.BlockSpec((1, MOBA_BLOCK, ATT_WIDTH), lambda b, i: (b, i, 0)),
                  pl.BlockSpec((1, t_len, ATT_WIDTH), lambda b, i: (b, 0, 0)),
                  pl.BlockSpec((1, nblk, N_HEADS * V_ROWS, MOBA_BLOCK), lambda b, i: (b, 0, 0, 0)),
                  pl.BlockSpec((1, nblk, ATT_WIDTH), lambda b, i: (b, 0, 0)),
                  pl.BlockSpec(memory_space=pltpu.SMEM)],
        out_specs=pl.BlockSpec((1, MOBA_BLOCK, ATT_WIDTH), lambda b, i: (b, i, 0)),
        out_shape=jax.ShapeDtypeStruct((n, t_len, ATT_WIDTH), F32),
        scratch_shapes=[pltpu.VMEM((N_HEADS, LANES, MOBA_BLOCK), BF16),
                        pltpu.VMEM((N_HEADS, nblk, MOBA_BLOCK), F32),
                        pltpu.VMEM((N_HEADS, SUBLANES, MOBA_BLOCK), F32),
                        pltpu.VMEM((N_HEADS, V_ROWS, MOBA_BLOCK), F32),
                        pltpu.VMEM((N_HEADS, MOBA_BLOCK, MOBA_BLOCK), F32)],
        compiler_params=_cparams("arbitrary", "arbitrary"),
        name="moba_prompt",
    )(q3, kb3, vta, kmean3, slopes)


PAGES_PER_STEP = 32
PAGES_PER_BLOCK = MOBA_BLOCK // PAGE_SIZE
BLOCKS_PER_STEP = PAGES_PER_STEP // PAGES_PER_BLOCK


def _sample_scan_kernel(pt_ref, *refs, n_past):
    pages = refs[:PAGES_PER_STEP]
    qt_ref, knt_ref, s_ref, idx_ref, qrep_sc, g_sc = refs[PAGES_PER_STEP:]
    n = pl.program_id(0)
    c = pl.program_id(1)
    lane = lax.broadcasted_iota(jnp.int32, (1, LANES), 1)

    @pl.when(c == 0)
    def _():
        qcol = jnp.sum(jnp.where(lane == n, qt_ref[...], 0.0), axis=-1, keepdims=True)
        qrep_sc[...] = jnp.broadcast_to(qcol, qrep_sc.shape)
        g_sc[...] = jnp.zeros_like(g_sc)

    qrep = qrep_sc[...].reshape(N_HEADS, HEAD_DIM, LANES)
    g = g_sc[...]
    for r in range(0, PAGES_PER_STEP, PAGES_PER_BLOCK):
        blk = jnp.zeros((N_HEADS, LANES), F32)
        for rr in range(r, r + PAGES_PER_BLOCK):
            sc = jnp.sum(pages[rr][0] * qrep, axis=1)
            s_ref[0, rr] = sc
            blk = blk + sc
        g = jnp.where(lane == c * BLOCKS_PER_STEP + r // PAGES_PER_BLOCK,
                      jnp.sum(blk, axis=-1, keepdims=True), g)
    g_sc[...] = g

    @pl.when(c == pl.num_programs(1) - 1)
    def _():
        kcol = jnp.sum(jnp.where(lane == n, knt_ref[...], 0.0), axis=-1, keepdims=True)
        g_own = jnp.sum((qrep_sc[:, 0:1] * kcol).reshape(N_HEADS, HEAD_DIM, 1), axis=1)
        gate = jnp.where(lane == n_past, g_own, g) * (1.0 / MOBA_BLOCK)
        gsel = jnp.where(lane < n_past, gate, -jnp.inf)
        lanef = lane.astype(F32)
        for k in range(MOBA_TOPK):
            mx = jnp.max(gsel, axis=-1, keepdims=True)
            am = jnp.min(jnp.where(gsel == mx, lanef, float(LANES)), axis=-1, keepdims=True)
            idx_ref[0, k] = jnp.broadcast_to(am.astype(jnp.int32), (N_HEADS, LANES))
            gsel = jnp.where(lanef == am, -jnp.inf, gsel)


def _sample_scan(page_table, ck, q_t, knew_t):
    n, n_pages = page_table.shape
    steps = n_pages // PAGES_PER_STEP
    n_past = n_pages // PAGES_PER_BLOCK

    def page_spec(r):
        return pl.BlockSpec((1, N_HEADS, HEAD_DIM, PAGE_SIZE),
                            lambda b, c, pt: (pt[b * n_pages + c * PAGES_PER_STEP + r], 0, 0, 0))

    full = lambda a: pl.BlockSpec(a.shape, lambda b, c, pt: (0,) * a.ndim)
    return pl.pallas_call(
        functools.partial(_sample_scan_kernel, n_past=n_past),
        grid_spec=pltpu.PrefetchScalarGridSpec(
            num_scalar_prefetch=1,
            grid=(n, steps),
            in_specs=[page_spec(r) for r in range(PAGES_PER_STEP)] + [full(q_t), full(knew_t)],
            out_specs=[pl.BlockSpec((1, PAGES_PER_STEP, N_HEADS, PAGE_SIZE), lambda b, c, pt: (b, c, 0, 0)),
                       pl.BlockSpec((1, MOBA_TOPK, N_HEADS, LANES), lambda b, c, pt: (b, 0, 0, 0))],
            scratch_shapes=[pltpu.VMEM((ATT_WIDTH, LANES), F32), pltpu.VMEM((N_HEADS, LANES), F32)],
        ),
        out_shape=[jax.ShapeDtypeStruct((n, n_pages, N_HEADS, PAGE_SIZE), F32),
                   jax.ShapeDtypeStruct((n, MOBA_TOPK, N_HEADS, LANES), jnp.int32)],
        compiler_params=_cparams("arbitrary", "arbitrary"),
        name="sample_scan",
    )(page_table.reshape(-1), *([ck] * PAGES_PER_STEP), q_t, knew_t)


SEL_PAGES = MOBA_TOPK * PAGES_PER_BLOCK


ATTEND_HEADS_PER_STEP = 4


def _sample_attend_kernel(pt_ref, ix_ref, slope_ref, s_ref, q_ref, knew_ref, vnew_ref, *refs, past_len):
    n_vt = ATTEND_HEADS_PER_STEP * SEL_PAGES
    vt_refs, o_ref = refs[:n_vt], refs[n_vt]
    n = pl.program_id(0)
    scale = HEAD_DIM ** -0.5
    off = lax.broadcasted_iota(jnp.int32, (1, PAGE_SIZE), 1).astype(F32)
    for hl in range(ATTEND_HEADS_PER_STEP):
        h = pl.program_id(1) * ATTEND_HEADS_PER_STEP + hl
        slope = slope_ref[h]
        qh = q_ref[0, pl.ds(h, 1), :]
        s0 = jnp.sum(qh * knew_ref[0, pl.ds(h, 1), :], axis=-1, keepdims=True) * scale
        rows = []
        for s in range(MOBA_TOPK):
            blk = ix_ref[(n * MOBA_TOPK + s) * N_HEADS + h]
            for half in range(PAGES_PER_BLOCK):
                raw = s_ref[0, blk * PAGES_PER_BLOCK + half, pl.ds(h, 1), :]
                dist = (past_len - blk * MOBA_BLOCK - half * PAGE_SIZE).astype(F32) - off
                rows.append(raw * scale - slope * dist)
        sc = jnp.concatenate(rows, axis=1)
        m = jnp.maximum(s0, jnp.max(sc, axis=-1, keepdims=True))
        p = jnp.exp(sc - m)
        p0 = jnp.exp(s0 - m)
        l = p0 + jnp.sum(p, axis=-1, keepdims=True)
        vt = jnp.concatenate([vt_refs[hl * SEL_PAGES + w][0, 0] for w in range(SEL_PAGES)], axis=1)
        pv = _dot_nt(jnp.broadcast_to(p, (SUBLANES, p.shape[1])).astype(BF16), vt.astype(BF16))[0:1]
        o_ref[0, pl.ds(h, 1), :] = (p0 * vnew_ref[0, pl.ds(h, 1), :] + pv) / l


def _sample_attend(page_table, idx, slopes, s_all, q3, knew3, vnew3, cv, past_len):
    n = q3.shape[0]
    n_pages = page_table.shape[1]

    def vt_spec(hl, w):
        def index(b, g, pt, ix, sl):
            h = g * ATTEND_HEADS_PER_STEP + hl
            blk = ix[(b * MOBA_TOPK + w // PAGES_PER_BLOCK) * N_HEADS + h]
            return (pt[b * n_pages + blk * PAGES_PER_BLOCK + w % PAGES_PER_BLOCK], h, 0, 0)
        return pl.BlockSpec((1, 1, HEAD_DIM, PAGE_SIZE), index)

    tok = pl.BlockSpec((1, N_HEADS, HEAD_DIM), lambda b, g, pt, ix, sl: (b, 0, 0))
    n_vt = ATTEND_HEADS_PER_STEP * SEL_PAGES
    return pl.pallas_call(
        functools.partial(_sample_attend_kernel, past_len=past_len),
        grid_spec=pltpu.PrefetchScalarGridSpec(
            num_scalar_prefetch=3,
            grid=(n, N_HEADS // ATTEND_HEADS_PER_STEP),
            in_specs=[pl.BlockSpec((1,) + s_all.shape[1:], lambda b, g, pt, ix, sl: (b, 0, 0, 0)), tok, tok, tok]
                     + [vt_spec(hl, w) for hl in range(ATTEND_HEADS_PER_STEP) for w in range(SEL_PAGES)],
            out_specs=tok,
        ),
        out_shape=jax.ShapeDtypeStruct((n, N_HEADS, HEAD_DIM), F32),
        compiler_params=_cparams("arbitrary", "arbitrary"),
        name="sample_attend",
    )(page_table.reshape(-1), idx.reshape(-1), slopes, s_all, q3, knew3, vnew3, *([cv] * n_vt))


def _merge_kernel(x_ref, ys_ref, ya_ref, gs_ref, ga_ref, wa_ref, wb_ref, wo_ref, o_ref):
    a = _dot(ys_ref[...].astype(BF16), wa_ref[...])
    b = _dot(ya_ref[...].astype(BF16), wb_ref[...])
    mix = jax.nn.sigmoid(gs_ref[...]) * a + jax.nn.sigmoid(ga_ref[...]) * b
    o_ref[...] = x_ref[...] + _dot(mix.astype(BF16), wo_ref[...])


def _merge(x2d, ys, ya, gs, ga, wa_b, wb_b, wo_b, tm):
    m = x2d.shape[0]
    row = lambda w: pl.BlockSpec((tm, w), lambda i: (i, 0))
    full = lambda a: pl.BlockSpec(a.shape, lambda i: (0,) * a.ndim)
    return pl.pallas_call(
        _merge_kernel,
        grid=(m // tm,),
        in_specs=[row(D_MODEL), row(SSM_WIDTH), row(ATT_WIDTH), row(D_MODEL), row(D_MODEL),
                  full(wa_b), full(wb_b), full(wo_b)],
        out_specs=row(D_MODEL),
        out_shape=jax.ShapeDtypeStruct((m, D_MODEL), F32),
        compiler_params=_cparams("arbitrary"),
        name="merge",
    )(x2d, ys, ya, gs, ga, wa_b, wb_b, wo_b)


FF_CHUNK = D_FF // 2


def _ffn_kernel(x_ref, *refs, tm, sequential):
    if sequential:
        (n2_ref, wup_ref, wgate_ref, cw_ref, cb_ref, wdown_ref, o_ref, cs_ref, tail) = refs
    else:
        (buf0_ref, buf1_ref, n2_ref, wup_ref, wgate_ref, cw_ref, cb_ref, wdown_ref, o_ref, cs0_ref, cs1_ref) = refs
    x = x_ref[...]
    ms = jnp.mean(x * x, axis=-1, keepdims=True)
    h = (x * lax.rsqrt(ms + NORM_EPS) * n2_ref[...]).astype(BF16)

    if sequential:
        @pl.when(pl.program_id(1) == 0)
        def _():
            tail[...] = jnp.zeros_like(tail)
        rows = lax.broadcasted_iota(jnp.int32, (tm, 1), 0)

    y = x
    for c in range(D_FF // FF_CHUNK):
        cs = slice(c * FF_CHUNK, (c + 1) * FF_CHUNK)
        up = _dot(h, wup_ref[:, cs])
        g = _dot(h, wgate_ref[:, cs])
        if sequential:
            p2 = tail[SUBLANES - 2:SUBLANES - 1, cs]
            p1 = tail[SUBLANES - 1:SUBLANES, cs]
            g1 = jnp.where(rows == 0, p1, pltpu.roll(g, 1, 0))
            g2 = jnp.where(rows == 0, p2, jnp.where(rows == 1, p1, pltpu.roll(g, 2, 0)))
            tail[:, cs] = g[tm - SUBLANES:, :]
            cs_ref[0, :, cs] = g[tm - 2:, :]
        else:
            g2 = buf0_ref[:, cs]
            g1 = buf1_ref[:, cs]
            cs0_ref[:, cs] = g1
            cs1_ref[:, cs] = g
        conv = cb_ref[:, cs] + cw_ref[0:1, cs] * g2 + cw_ref[1:2, cs] * g1 + cw_ref[2:3, cs] * g
        act = (jax.nn.gelu(conv) * up).astype(BF16)
        y = y + _dot(act, wdown_ref[cs, :])
    o_ref[...] = y


def _ffn_prompt(x2d, n2, wup_b, wgate_b, cw, cb, wdown_b, n_seq, t_len, tm):
    nt = t_len // tm
    full = lambda a: pl.BlockSpec(a.shape, lambda b, t: (0,) * a.ndim)
    row = pl.BlockSpec((tm, D_MODEL), lambda b, t: (b * nt + t, 0))
    return pl.pallas_call(
        functools.partial(_ffn_kernel, tm=tm, sequential=True),
        grid=(n_seq, nt),
        in_specs=[row, full(n2), full(wup_b), full(wgate_b), full(cw), full(cb), full(wdown_b)],
        out_specs=[row, pl.BlockSpec((1, 2, D_FF), lambda b, t: (b, 0, 0))],
        out_shape=[jax.ShapeDtypeStruct((n_seq * t_len, D_MODEL), F32),
                   jax.ShapeDtypeStruct((n_seq, 2, D_FF), F32)],
        scratch_shapes=[pltpu.VMEM((SUBLANES, D_FF), F32)],
        compiler_params=_cparams("arbitrary", "arbitrary"),
        name="ffn_prompt",
    )(x2d, n2, wup_b, wgate_b, cw, cb, wdown_b)


def _ffn_step(x2d, buf0, buf1, n2, wup_b, wgate_b, cw, cb, wdown_b):
    m = x2d.shape[0]
    args = (x2d, buf0, buf1, n2, wup_b, wgate_b, cw, cb, wdown_b)
    full = lambda a: pl.BlockSpec(a.shape, lambda i: (0,) * a.ndim)
    return pl.pallas_call(
        functools.partial(_ffn_kernel, tm=m, sequential=False),
        grid=(1,),
        in_specs=[full(a) for a in args],
        out_specs=[pl.BlockSpec((m, D_MODEL), lambda i: (0, 0)),
                   pl.BlockSpec((m, D_FF), lambda i: (0, 0)),
                   pl.BlockSpec((m, D_FF), lambda i: (0, 0))],
        out_shape=[jax.ShapeDtypeStruct((m, D_MODEL), F32),
                   jax.ShapeDtypeStruct((m, D_FF), F32),
                   jax.ShapeDtypeStruct((m, D_FF), F32)],
        compiler_params=_cparams("arbitrary"),
        name="ffn_step",
    )(*args)


def _s5_params(lam_re, lam_im, log_dt, b_re, b_im, c_re, c_im):
    lr, li = lam_re.astype(F32), lam_im.astype(F32)
    dt = jnp.exp(log_dt.astype(F32))[:, None]
    mag = jnp.exp(lr * dt)
    ar, ai = mag * jnp.cos(li * dt), mag * jnp.sin(li * dt)
    den = lr * lr + li * li
    fr = ((ar - 1.0) * lr + ai * li) / den
    fi = (ai * lr - (ar - 1.0) * li) / den
    br_, bi_ = b_re.astype(F32), b_im.astype(F32)
    bbr = fr[..., None] * br_ - fi[..., None] * bi_
    bbi = fr[..., None] * bi_ + fi[..., None] * br_
    eye = jnp.eye(SSM_GROUPS, dtype=F32)

    def bdiag_in(w):
        return jnp.einsum('gsc,gh->gchs', w, eye).reshape(SSM_WIDTH, N_STATE)

    def bdiag_out(w):
        return jnp.einsum('gcs,gh->gshc', w, eye).reshape(N_STATE, SSM_WIDTH)

    b_re_d, b_im_d = bdiag_in(bbr), bdiag_in(bbi)
    c_re_d, c_im_d = bdiag_out(c_re.astype(F32)), -bdiag_out(c_im.astype(F32))
    bblk = jnp.concatenate([b_re_d, b_im_d], axis=1).astype(BF16)
    cblk = jnp.concatenate([c_re_d, c_im_d], axis=0).astype(BF16)
    n_in = S5_LANE_CHUNK // SSM_STATE * SSM_GROUP
    chunks = range(N_STATE // S5_LANE_CHUNK)
    cut = lambda m, c: m[c * n_in:(c + 1) * n_in, c * S5_LANE_CHUNK:(c + 1) * S5_LANE_CHUNK]
    cut_t = lambda m, c: m[c * S5_LANE_CHUNK:(c + 1) * S5_LANE_CHUNK, c * n_in:(c + 1) * n_in]
    bb4 = jnp.stack([jnp.concatenate([cut(b_re_d, c), cut(b_im_d, c)], axis=1) for c in chunks]).astype(BF16)
    cc4 = jnp.stack([jnp.concatenate([cut_t(c_re_d, c), cut_t(c_im_d, c)], axis=0) for c in chunks]).astype(BF16)

    a_re, a_im = ar.reshape(1, N_STATE), ai.reshape(1, N_STATE)

    def cmul(x, y):
        return x[0] * y[0] - x[1] * y[1], x[0] * y[1] + x[1] * y[0]

    pows = [(a_re, a_im)]
    for _ in range(SUBLANES - 1):
        pows.append(cmul(pows[-1], (a_re, a_im)))
    rows = jnp.arange(SUBLANES)[:, None]
    tabs = []
    for s in (1, 2, 4):
        tabs.append(jnp.where(rows >= s, pows[s - 1][0], 0.0))
        tabs.append(jnp.where(rows >= s, pows[s - 1][1], 0.0))
    tabs.append(jnp.concatenate([p[0] for p in pows], axis=0))
    tabs.append(jnp.concatenate([p[1] for p in pows], axis=0))
    return a_re, a_im, bblk, cblk, bb4, cc4, jnp.stack(tabs)


def kernel(x_prompt, x_sample, cache_k, cache_v, state_ssm_re, state_ssm_im, state_conv, page_table,
           norm1_w, w_in, q_norm_w, k_norm_w, lam_re, lam_im, log_dt, b_re, b_im, c_re, c_im, d_skip,
           w_glu, b_glu, w_a, w_b, w_o, norm2_w, w_up, w_gate, conv_w, conv_b, w_down):
    depth = norm1_w.shape[0]
    batch, seq, _ = x_prompt.shape
    dec_batch, dec_seq, _ = x_sample.shape
    assert depth == 1 and dec_seq == 1 and seq % MOBA_BLOCK == 0 and dec_batch <= LANES
    past_len = page_table.shape[1] * PAGE_SIZE
    slopes = 2.0 ** (-8.0 * jnp.arange(1, N_HEADS + 1, dtype=F32) / N_HEADS)
    seg = jnp.kron(jnp.eye(N_HEADS, dtype=F32), jnp.ones((HEAD_DIM, HEAD_DIM), F32)).astype(BF16)

    l = 0
    row = lambda a: a.reshape(1, -1).astype(F32)
    n1, n2 = row(norm1_w[l]), row(norm2_w[l])
    qn_row = row(jnp.tile(q_norm_w[l], N_HEADS))
    kn_row = row(jnp.tile(k_norm_w[l], N_HEADS))
    win_b = w_in[l].astype(BF16)
    a_re, a_im, bblk, cblk, bb4, cc4, tabs = _s5_params(lam_re[l], lam_im[l], log_dt[l], b_re[l], b_im[l], c_re[l], c_im[l])
    dskip, bglu = row(d_skip[l]), row(b_glu[l])
    wglu_b = w_glu[l].astype(BF16)
    wa_b, wb_b, wo_b = w_a[l].astype(BF16), w_b[l].astype(BF16), w_o[l].astype(BF16)
    wup_b, wgate_b, wdown_b = w_up[l].astype(BF16), w_gate[l].astype(BF16), w_down[l].astype(BF16)
    cw, cb = conv_w[l].astype(F32), row(conv_b[l])

    mp = batch * seq
    xp = x_prompt.reshape(mp, D_MODEL)
    u, q, gs, ga, k_t, v_t, kb, vta, kmean = _inproj(xp, n1, win_b, qn_row, kn_row, seg, batch, seq, 512, True)
    y_ssm, hre_p, him_p = _s5_prompt(u, bb4, cc4, tabs, dskip, wglu_b, bglu, batch, seq, 256)
    y_att = _moba_prompt(q.reshape(batch, seq, ATT_WIDTH), kb.reshape(batch, seq, ATT_WIDTH), vta,
                         kmean.reshape(batch, seq // MOBA_BLOCK, ATT_WIDTH), slopes)
    x1 = _merge(xp, y_ssm, y_att.reshape(mp, ATT_WIDTH), gs, ga, wa_b, wb_b, wo_b, 512)
    yp, conv_p = _ffn_prompt(x1, n2, wup_b, wgate_b, cw, cb, wdown_b, batch, seq, 512)

    xs = x_sample.reshape(dec_batch, D_MODEL)
    u_s, q_s, gs_s, ga_s, k_s, v_s = _inproj(xs, n1, win_b, qn_row, kn_row, seg, 1, dec_batch, dec_batch, False)
    y_ssm_s, hre_s, him_s = _s5_step(u_s, state_ssm_re[l].reshape(dec_batch, N_STATE),
                                     state_ssm_im[l].reshape(dec_batch, N_STATE),
                                     a_re, a_im, bblk, cblk, dskip, wglu_b, bglu)
    heads = lambda a: a.reshape(dec_batch, N_HEADS, HEAD_DIM)
    pad = lambda a_t: jnp.pad(a_t, ((0, 0), (0, LANES - dec_batch)))
    ck = jnp.transpose(cache_k[l], (0, 2, 3, 1))
    cv = jnp.transpose(cache_v[l], (0, 2, 3, 1))
    s_all, idx = _sample_scan(page_table, ck, pad(q_s.T), pad(k_s.T))
    y_att_s = _sample_attend(page_table, idx[..., 0], slopes, s_all, heads(q_s), heads(k_s), heads(v_s), cv, past_len)
    x1_s = _merge(xs, y_ssm_s, y_att_s.reshape(dec_batch, ATT_WIDTH), gs_s, ga_s, wa_b, wb_b, wo_b, dec_batch)
    ys, cs0, cs1 = _ffn_step(x1_s, state_conv[l, :, 0, :], state_conv[l, :, 1, :], n2, wup_b, wgate_b, cw, cb, wdown_b)

    g, s = SSM_GROUPS, SSM_STATE
    kv_out = lambda a_t: jnp.transpose(a_t.reshape(batch, N_HEADS, HEAD_DIM, seq), (0, 3, 1, 2))[None]
    return (yp.reshape(batch, seq, D_MODEL),
            ys.reshape(dec_batch, 1, D_MODEL),
            kv_out(k_t),
            kv_out(v_t),
            hre_p.reshape(1, batch, g, s),
            him_p.reshape(1, batch, g, s),
            conv_p.reshape(1, batch, 2, D_FF),
            k_s.reshape(1, dec_batch, 1, N_HEADS, HEAD_DIM),
            v_s.reshape(1, dec_batch, 1, N_HEADS, HEAD_DIM),
            hre_s.reshape(1, dec_batch, g, s),
            him_s.reshape(1, dec_batch, g, s),
            jnp.stack([cs0, cs1], axis=1).reshape(1, dec_batch, 2, D_FF))
```

```python
import functools

import jax
import jax.numpy as jnp
from jax import lax
from jax.experimental import pallas as pl
from jax.experimental.pallas import tpu as pltpu

F32 = jnp.float32
BF16 = jnp.bfloat16

D_MODEL = 1024
SSM_WIDTH = 512
SSM_GROUP = 16
SSM_GROUPS = 32
SSM_STATE = 64
N_STATE = SSM_GROUPS * SSM_STATE
N_HEADS = 8
HEAD_DIM = 64
ATT_WIDTH = 512
MOBA_BLOCK = 256
MOBA_TOPK = 3
D_FF = 2816
NORM_EPS = 1e-6
PAGE_SIZE = 128
LOG2E = 1.4426950408889634
NEG = -1e30

VMEM_LIMIT_BYTES = 56 * 1024 * 1024
SUBLANES = 8
LANES = 128


def _cparams(*sem):
    return pltpu.CompilerParams(dimension_semantics=sem, vmem_limit_bytes=VMEM_LIMIT_BYTES)


def _dot(a, b):
    return jnp.dot(a, b, preferred_element_type=F32)


def _dot_nt(a, b):
    return lax.dot_general(a, b, (((1,), (1,)), ((), ())), preferred_element_type=F32)


def _split_bf16(a):
    hi = a.astype(BF16)
    lo = (a - hi.astype(F32)).astype(BF16)
    return hi, lo


V_ROWS = 80


def _inproj_kernel(x_ref, n1_ref, win_ref, qn_ref, kn_ref, seg_ref, u_ref, q_ref, gs_ref, ga_ref, *rest,
                   tm, prompt):
    x = x_ref[...]
    ms = jnp.mean(x * x, axis=-1, keepdims=True)
    h = (x * lax.rsqrt(ms + NORM_EPS) * n1_ref[...]).astype(BF16)

    def proj(lo, hi):
        return _dot(h, win_ref[:, lo:hi])

    def head_rmsnorm(z, w_row):
        sq_hi, sq_lo = _split_bf16(z * z)
        ss = _dot(sq_hi, seg_ref[...]) + _dot(sq_lo, seg_ref[...])
        return z * lax.rsqrt(ss * (1.0 / HEAD_DIM) + NORM_EPS) * w_row

    o1 = SSM_WIDTH
    o2 = o1 + ATT_WIDTH
    o3 = o2 + ATT_WIDTH
    o4 = o3 + ATT_WIDTH
    o5 = o4 + D_MODEL
    u_ref[...] = proj(0, o1)
    q_ref[...] = head_rmsnorm(proj(o1, o2), qn_ref[...])
    k = head_rmsnorm(proj(o2, o3), kn_ref[...])
    v = proj(o3, o4)
    gs_ref[...] = proj(o4, o5)
    ga_ref[...] = proj(o5, o5 + D_MODEL)
    if not prompt:
        k_ref, v_ref = rest
        k_ref[...] = k
        v_ref[...] = v
        return
    kt_ref, vt_ref, kb_ref, vta_ref, kmean_ref = rest
    nb = tm // MOBA_BLOCK
    kt_ref[0] = k.T
    vt = v.T
    vt_ref[0] = vt
    kb_ref[...] = k.astype(BF16)
    ones = jnp.ones((V_ROWS - HEAD_DIM, MOBA_BLOCK), BF16)
    for hd in range(N_HEADS):
        for b in range(nb):
            vta_ref[0, b, hd * V_ROWS:hd * V_ROWS + HEAD_DIM, :] = (
                vt[hd * HEAD_DIM:(hd + 1) * HEAD_DIM, b * MOBA_BLOCK:(b + 1) * MOBA_BLOCK].astype(BF16))
            vta_ref[0, b, hd * V_ROWS + HEAD_DIM:(hd + 1) * V_ROWS, :] = ones
    ksum = k.reshape(nb, MOBA_BLOCK, ATT_WIDTH).sum(axis=1)
    kmean_ref[...] = (ksum * (1.0 / MOBA_BLOCK)).reshape(nb, 1, ATT_WIDTH)


def _inproj(x2d, n1, win_b, qn_row, kn_row, seg, n_seq, t_len, tm, prompt):
    nt = t_len // tm
    m = n_seq * t_len
    row = lambda w: pl.BlockSpec((tm, w), lambda b, t: (b * nt + t, 0))
    full = lambda a: pl.BlockSpec(a.shape, lambda b, t: (0,) * a.ndim)
    f32 = lambda *s: jax.ShapeDtypeStruct(s, F32)
    out_shape = [f32(m, SSM_WIDTH), f32(m, ATT_WIDTH), f32(m, D_MODEL), f32(m, D_MODEL)]
    out_specs = [row(SSM_WIDTH), row(ATT_WIDTH), row(D_MODEL), row(D_MODEL)]
    if prompt:
        nb = tm // MOBA_BLOCK
        tspec = pl.BlockSpec((1, ATT_WIDTH, tm), lambda b, t: (b, 0, t))
        out_shape += [f32(n_seq, ATT_WIDTH, t_len), f32(n_seq, ATT_WIDTH, t_len),
                      jax.ShapeDtypeStruct((m, ATT_WIDTH), BF16),
                      jax.ShapeDtypeStruct((n_seq, t_len // MOBA_BLOCK, N_HEADS * V_ROWS, MOBA_BLOCK), BF16),
                      f32(m // MOBA_BLOCK, 1, ATT_WIDTH)]
        out_specs += [tspec, tspec, row(ATT_WIDTH),
                      pl.BlockSpec((1, nb, N_HEADS * V_ROWS, MOBA_BLOCK), lambda b, t: (b, t, 0, 0)),
                      pl.BlockSpec((nb, 1, ATT_WIDTH), lambda b, t: (b * nt + t, 0, 0))]
    else:
        out_shape += [f32(m, ATT_WIDTH), f32(m, ATT_WIDTH)]
        out_specs += [row(ATT_WIDTH), row(ATT_WIDTH)]
    return pl.pallas_call(
        functools.partial(_inproj_kernel, tm=tm, prompt=prompt),
        grid=(n_seq, nt),
        in_specs=[row(D_MODEL), full(n1), full(win_b), full(qn_row), full(kn_row), full(seg)],
        out_specs=out_specs,
        out_shape=out_shape,
        compiler_params=_cparams("arbitrary", "arbitrary"),
        name="inproj",
    )(x2d, n1, win_b, qn_row, kn_row, seg)


def _s5_output(y_lin, u, dskip_ref, wglu_ref, bglu_ref):
    y = jax.nn.gelu(y_lin + dskip_ref[...] * u)
    gate = jax.nn.sigmoid(_dot(y.astype(BF16), wglu_ref[...]) + bglu_ref[...])
    return y * gate


S5_LANE_CHUNK = 512


def _s5_prompt_kernel(u_ref, bb_ref, cc_ref, tabs_ref, dskip_ref, wglu_ref, bglu_ref,
                      y_ref, hre_ref, him_ref, xre, xim, cre, cim, *, tt):
    t = pl.program_id(1)

    @pl.when(t == 0)
    def _():
        cre[...] = jnp.zeros_like(cre)
        cim[...] = jnp.zeros_like(cim)

    u = u_ref[...]
    ub = u.astype(BF16)
    n_in = S5_LANE_CHUNK // SSM_STATE * SSM_GROUP
    y_parts = []
    for c in range(N_STATE // S5_LANE_CHUNK):
        ls = slice(c * S5_LANE_CHUNK, (c + 1) * S5_LANE_CHUNK)
        us = ub[:, c * n_in:(c + 1) * n_in]
        xre[:, ls] = _dot(us, bb_ref[c, :, :S5_LANE_CHUNK])
        xim[:, ls] = _dot(us, bb_ref[c, :, S5_LANE_CHUNK:])
        cr, ci = cre[:, ls], cim[:, ls]
        for b in range(tt // SUBLANES):
            rs = slice(b * SUBLANES, (b + 1) * SUBLANES)
            xr = xre[rs, ls]
            xi = xim[rs, ls]
            for k, s in enumerate((1, 2, 4)):
                pr = tabs_ref[2 * k, :, ls]
                pi = tabs_ref[2 * k + 1, :, ls]
                rr = pltpu.roll(xr, s, 0)
                ri = pltpu.roll(xi, s, 0)
                xr, xi = xr + (pr * rr - pi * ri), xi + (pr * ri + pi * rr)
            pr = tabs_ref[6, :, ls]
            pi = tabs_ref[7, :, ls]
            hr = xr + (pr * cr - pi * ci)
            hi = xi + (pr * ci + pi * cr)
            xre[rs, ls] = hr
            xim[rs, ls] = hi
            shape = (SUBLANES, S5_LANE_CHUNK)
            cr = jnp.broadcast_to(hr[SUBLANES - 1:SUBLANES, :], shape)
            ci = jnp.broadcast_to(hi[SUBLANES - 1:SUBLANES, :], shape)
        cre[:, ls] = cr
        cim[:, ls] = ci
        y_parts.append(_dot(xre[:, ls].astype(BF16), cc_ref[c, :S5_LANE_CHUNK, :])
                       + _dot(xim[:, ls].astype(BF16), cc_ref[c, S5_LANE_CHUNK:, :]))

    hre_ref[0] = cre[0:1, :]
    him_ref[0] = cim[0:1, :]
    y_ref[...] = _s5_output(jnp.concatenate(y_parts, axis=1), u, dskip_ref, wglu_ref, bglu_ref)


def _s5_prompt(u2d, bblk, cblk, tabs, dskip, wglu_b, bglu, n_seq, t_len, tt):
    nt = t_len // tt
    full = lambda a: pl.BlockSpec(a.shape, lambda b, t: (0,) * a.ndim)
    return pl.pallas_call(
        functools.partial(_s5_prompt_kernel, tt=tt),
        grid=(n_seq, nt),
        in_specs=[pl.BlockSpec((tt, SSM_WIDTH), lambda b, t: (b * nt + t, 0)),
                  full(bblk), full(cblk), full(tabs), full(dskip), full(wglu_b), full(bglu)],
        out_specs=[pl.BlockSpec((tt, SSM_WIDTH), lambda b, t: (b * nt + t, 0)),
                   pl.BlockSpec((1, 1, N_STATE), lambda b, t: (b, 0, 0)),
                   pl.BlockSpec((1, 1, N_STATE), lambda b, t: (b, 0, 0))],
        out_shape=[jax.ShapeDtypeStruct((n_seq * t_len, SSM_WIDTH), F32),
                   jax.ShapeDtypeStruct((n_seq, 1, N_STATE), F32),
                   jax.ShapeDtypeStruct((n_seq, 1, N_STATE), F32)],
        scratch_shapes=[pltpu.VMEM((tt, N_STATE), F32), pltpu.VMEM((tt, N_STATE), F32),
                        pltpu.VMEM((SUBLANES, N_STATE), F32), pltpu.VMEM((SUBLANES, N_STATE), F32)],
        compiler_params=_cparams("arbitrary", "arbitrary"),
        name="s5_prompt",
    )(u2d, bblk, cblk, tabs, dskip, wglu_b, bglu)


def _s5_step_kernel(u_ref, h0re_ref, h0im_ref, are_ref, aim_ref, bblk_ref, cblk_ref,
                    dskip_ref, wglu_ref, bglu_ref, y_ref, hre_ref, him_ref):
    u = u_ref[...]
    ub = u.astype(BF16)
    ar = are_ref[...]
    ai = aim_ref[...]
    h0r = h0re_ref[...]
    h0i = h0im_ref[...]
    hr = _dot(ub, bblk_ref[:, :N_STATE]) + (ar * h0r - ai * h0i)
    hi = _dot(ub, bblk_ref[:, N_STATE:]) + (ar * h0i + ai * h0r)
    hre_ref[...] = hr
    him_ref[...] = hi
    y_lin = _dot(hr.astype(BF16), cblk_ref[:N_STATE, :]) + _dot(hi.astype(BF16), cblk_ref[N_STATE:, :])
    y_ref[...] = _s5_output(y_lin, u, dskip_ref, wglu_ref, bglu_ref)


def _s5_step(u2d, h0re, h0im, are, aim, bblk, cblk, dskip, wglu_b, bglu):
    m = u2d.shape[0]
    args = (u2d, h0re, h0im, are, aim, bblk, cblk, dskip, wglu_b, bglu)
    full = lambda a: pl.BlockSpec(a.shape, lambda i: (0,) * a.ndim)
    return pl.pallas_call(
        _s5_step_kernel,
        grid=(1,),
        in_specs=[full(a) for a in args],
        out_specs=[pl.BlockSpec((m, SSM_WIDTH), lambda i: (0, 0)),
                   pl.BlockSpec((m, N_STATE), lambda i: (0, 0)),
                   pl.BlockSpec((m, N_STATE), lambda i: (0, 0))],
        out_shape=[jax.ShapeDtypeStruct((m, SSM_WIDTH), F32),
                   jax.ShapeDtypeStruct((m, N_STATE), F32),
                   jax.ShapeDtypeStruct((m, N_STATE), F32)],
        compiler_params=_cparams("arbitrary"),
        name="s5_step",
    )(*args)


def _topk_bias_t(gate_t, nblk):
    bidx = lax.broadcasted_iota(jnp.int32, gate_t.shape, 0).astype(F32)
    bias = jnp.full(gate_t.shape, NEG, F32)
    g = gate_t
    for _ in range(MOBA_TOPK):
        mx = jnp.max(g, axis=0, keepdims=True)
        am = jnp.min(jnp.where(g == mx, bidx, float(nblk)), axis=0, keepdims=True)
        hit = bidx == am
        bias = jnp.where(hit & (mx > -jnp.inf), 0.0, bias)
        g = jnp.where(hit, -jnp.inf, g)
    return bias


ALIBI_PARTS = 3
SCORE_LOOKAHEAD = 6
PAST_BLOCKS_PER_TRIP = 4


def _moba_prompt_kernel(q_ref, kb_ref, vta_ref, kmean_ref, slope_ref, o_ref,
                        qt_sc, sel_sc, m_sc, acc_sc, ak_sc, *, nblk):
    i = pl.program_id(1)
    bq = MOBA_BLOCK
    key_idx = lax.broadcasted_iota(jnp.int32, (bq, bq), 0)
    qry_idx = lax.broadcasted_iota(jnp.int32, (bq, bq), 1)
    lane = lax.broadcasted_iota(jnp.int32, (1, LANES), 1)
    sub = lax.broadcasted_iota(jnp.int32, (LANES, 1), 0)
    gblk = lax.broadcasted_iota(jnp.int32, (nblk, bq), 0)


    @pl.when((pl.program_id(0) == 0) & (i == 0))
    def _():
        key_pos = lax.broadcasted_iota(jnp.int32, (bq, LANES), 0).astype(F32)
        lane_b = lax.broadcasted_iota(jnp.int32, (bq, LANES), 1)
        for p in range(N_HEADS // 2):
            slope = jnp.where(lane_b // ALIBI_PARTS == 0, slope_ref[2 * p], slope_ref[2 * p + 1]) * LOG2E
            rest = slope * key_pos
            tile = jnp.zeros((bq, LANES), F32)
            for k in range(ALIBI_PARTS):
                part = rest.astype(BF16).astype(F32)
                tile = jnp.where(lane_b % ALIBI_PARTS == k, part, tile)
                rest = rest - part
            ak_sc[p] = jnp.where(lane_b < 2 * ALIBI_PARTS, tile, 0.0).astype(BF16)
        for h in range(N_HEADS):
            qt_sc[h, LANES:, :] = jnp.broadcast_to(
                jnp.where((sub // ALIBI_PARTS) == h % 2, 1.0, 0.0).astype(BF16), (LANES, bq))

    for p in range(N_HEADS // 2):
        ls = slice(p * LANES, (p + 1) * LANES)
        qp = q_ref[0, :, ls]
        q_hi, q_lo = _split_bf16(qp)
        qt = qp.T * (HEAD_DIM ** -0.5 * LOG2E)
        km = kmean_ref[0, :, ls]
        for hh in range(2):
            h = 2 * p + hh
            qt_sc[h, :LANES, :] = jnp.where((sub // HEAD_DIM) == hh, qt, 0.0).astype(BF16)
            km_hi, km_lo = _split_bf16(jnp.where((lane // HEAD_DIM) == hh, km, 0.0))
            gate_t = _dot_nt(km_hi, q_hi) + _dot_nt(km_hi, q_lo) + _dot_nt(km_lo, q_hi)
            sel_sc[h] = _topk_bias_t(jnp.where(gblk < i, gate_t, -jnp.inf), nblk)

    def scores(j, h):
        c0 = pl.multiple_of(j * bq, bq)
        kj = kb_ref[0, pl.ds(c0, bq), (h // 2) * LANES:(h // 2 + 1) * LANES]
        return _dot(jnp.concatenate([kj, ak_sc[h // 2]], axis=1), qt_sc[h])

    def values(j, h):
        return vta_ref[0, j, h * V_ROWS:(h + 1) * V_ROWS, :]

    def pipelined(blocks, update_fn):
        tasks = [(j, h) for j in blocks for h in range(N_HEADS)]
        pending = [scores(*t) for t in tasks[:SCORE_LOOKAHEAD]]
        for k, (j, h) in enumerate(tasks):
            if k + SCORE_LOOKAHEAD < len(tasks):
                pending.append(scores(*tasks[k + SCORE_LOOKAHEAD]))
            update_fn(j, h, pending.pop(0))

    def own_update(j, h, s_t):
        s_t = jnp.where(key_idx <= qry_idx, s_t, NEG)
        m = jnp.max(s_t, axis=0, keepdims=True)
        m_sc[h] = jnp.broadcast_to(m, (SUBLANES, bq))
        acc_sc[h] = _dot(values(j, h), jnp.exp2(s_t - m).astype(BF16))

    pipelined([i], own_update)

    def past_update(j, h, s_t):
        bias = sel_sc[h, pl.ds(j, 1), :] + (slope_ref[h] * LOG2E) * ((j - i) * bq).astype(F32)
        m_old = m_sc[h][0:1, :]
        m_new = jnp.maximum(m_old, jnp.max(s_t, axis=0, keepdims=True) + bias)
        m_sc[h] = jnp.broadcast_to(m_new, (SUBLANES, bq))
        pv = _dot(values(j, h), jnp.exp2(s_t - (m_new - bias)).astype(BF16))
        acc_sc[h] = jnp.exp2(m_old - m_new) * acc_sc[h] + pv

    def body(t, carry):
        pipelined([PAST_BLOCKS_PER_TRIP * t + d for d in range(PAST_BLOCKS_PER_TRIP)], past_update)
        return carry

    n_trips = i // PAST_BLOCKS_PER_TRIP
    lax.fori_loop(0, n_trips, body, 0)
    j_left = n_trips * PAST_BLOCKS_PER_TRIP
    width = PAST_BLOCKS_PER_TRIP // 2
    while width:
        @pl.when((i & width) != 0)
        def _(j_left=j_left, width=width):
            pipelined([j_left + d for d in range(width)], past_update)

        j_left = j_left + (i & width)
        width //= 2

    for p in range(N_HEADS // 2):
        halves = []
        for h in (2 * p, 2 * p + 1):
            acc = acc_sc[h]
            halves.append(acc[:HEAD_DIM, :] / acc[HEAD_DIM:HEAD_DIM + 1, :])
        o_ref[0, :, p * LANES:(p + 1) * LANES] = jnp.concatenate(halves, axis=0).T


def _moba_prompt(q3, kb3, vta, kmean3, slopes):
    n, t_len, _ = q3.shape
    nblk = t_len // MOBA_BLOCK
    return pl.pallas_call(
        functools.partial(_moba_prompt_kernel, nblk=nblk),
        grid=(n, nblk),
        in_specs=[pl.BlockSpec((1, MOBA_BLOCK, ATT_WIDTH), lambda b, i: (b, i, 0)),
                  pl.BlockSpec((1, t_len, ATT_WIDTH), lambda b, i: (b, 0, 0)),
                  pl.BlockSpec((1, nblk, N_HEADS * V_ROWS, MOBA_BLOCK), lambda b, i: (b, 0, 0, 0)),
                  pl.BlockSpec((1, nblk, ATT_WIDTH), lambda b, i: (b, 0, 0)),
                  pl.BlockSpec(memory_space=pltpu.SMEM)],
        out_specs=pl.BlockSpec((1, MOBA_BLOCK, ATT_WIDTH), lambda b, i: (b, i, 0)),
        out_shape=jax.ShapeDtypeStruct((n, t_len, ATT_WIDTH), F32),
        scratch_shapes=[pltpu.VMEM((N_HEADS, 2 * LANES, MOBA_BLOCK), BF16),
                        pltpu.VMEM((N_HEADS, nblk, MOBA_BLOCK), F32),
                        pltpu.VMEM((N_HEADS, SUBLANES, MOBA_BLOCK), F32),
                        pltpu.VMEM((N_HEADS, V_ROWS, MOBA_BLOCK), F32),
                        pltpu.VMEM((N_HEADS // 2, MOBA_BLOCK, LANES), BF16)],
        compiler_params=_cparams("arbitrary", "arbitrary"),
        name="moba_prompt",
    )(q3, kb3, vta, kmean3, slopes)


PAGES_PER_STEP = 32
PAGES_PER_BLOCK = MOBA_BLOCK // PAGE_SIZE
BLOCKS_PER_STEP = PAGES_PER_STEP // PAGES_PER_BLOCK


def _sample_scan_kernel(pt_ref, *refs, n_past):
    pages = refs[:PAGES_PER_STEP]
    qt_ref, knt_ref, s_ref, idx_ref, qrep_sc, g_sc = refs[PAGES_PER_STEP:]
    n = pl.program_id(0)
    c = pl.program_id(1)
    lane = lax.broadcasted_iota(jnp.int32, (1, LANES), 1)

    @pl.when(c == 0)
    def _():
        qcol = jnp.sum(jnp.where(lane == n, qt_ref[...], 0.0), axis=-1, keepdims=True)
        qrep_sc[...] = jnp.broadcast_to(qcol, qrep_sc.shape)
        g_sc[...] = jnp.zeros_like(g_sc)

    qrep = qrep_sc[...].reshape(N_HEADS, HEAD_DIM, LANES)
    g = g_sc[...]
    for r in range(0, PAGES_PER_STEP, PAGES_PER_BLOCK):
        blk = jnp.zeros((N_HEADS, LANES), F32)
        for rr in range(r, r + PAGES_PER_BLOCK):
            sc = jnp.sum(pages[rr][0] * qrep, axis=1)
            s_ref[0, rr] = sc
            blk = blk + sc
        g = jnp.where(lane == c * BLOCKS_PER_STEP + r // PAGES_PER_BLOCK,
                      jnp.sum(blk, axis=-1, keepdims=True), g)
    g_sc[...] = g

    @pl.when(c == pl.num_programs(1) - 1)
    def _():
        kcol = jnp.sum(jnp.where(lane == n, knt_ref[...], 0.0), axis=-1, keepdims=True)
        g_own = jnp.sum((qrep_sc[:, 0:1] * kcol).reshape(N_HEADS, HEAD_DIM, 1), axis=1)
        gate = jnp.where(lane == n_past, g_own, g) * (1.0 / MOBA_BLOCK)
        gsel = jnp.where(lane < n_past, gate, -jnp.inf)
        lanef = lane.astype(F32)
        for k in range(MOBA_TOPK):
            mx = jnp.max(gsel, axis=-1, keepdims=True)
            am = jnp.min(jnp.where(gsel == mx, lanef, float(LANES)), axis=-1, keepdims=True)
            idx_ref[0, k] = jnp.broadcast_to(am.astype(jnp.int32), (N_HEADS, LANES))
            gsel = jnp.where(lanef == am, -jnp.inf, gsel)


def _sample_scan(page_table, ck, q_t, knew_t):
    n, n_pages = page_table.shape
    steps = n_pages // PAGES_PER_STEP
    n_past = n_pages // PAGES_PER_BLOCK

    def page_spec(r):
        return pl.BlockSpec((1, N_HEADS, HEAD_DIM, PAGE_SIZE),
                            lambda b, c, pt: (pt[b * n_pages + c * PAGES_PER_STEP + r], 0, 0, 0))

    full = lambda a: pl.BlockSpec(a.shape, lambda b, c, pt: (0,) * a.ndim)
    return pl.pallas_call(
        functools.partial(_sample_scan_kernel, n_past=n_past),
        grid_spec=pltpu.PrefetchScalarGridSpec(
            num_scalar_prefetch=1,
            grid=(n, steps),
            in_specs=[page_spec(r) for r in range(PAGES_PER_STEP)] + [full(q_t), full(knew_t)],
            out_specs=[pl.BlockSpec((1, PAGES_PER_STEP, N_HEADS, PAGE_SIZE), lambda b, c, pt: (b, c, 0, 0)),
                       pl.BlockSpec((1, MOBA_TOPK, N_HEADS, LANES), lambda b, c, pt: (b, 0, 0, 0))],
            scratch_shapes=[pltpu.VMEM((ATT_WIDTH, LANES), F32), pltpu.VMEM((N_HEADS, LANES), F32)],
        ),
        out_shape=[jax.ShapeDtypeStruct((n, n_pages, N_HEADS, PAGE_SIZE), F32),
                   jax.ShapeDtypeStruct((n, MOBA_TOPK, N_HEADS, LANES), jnp.int32)],
        compiler_params=_cparams("arbitrary", "arbitrary"),
        name="sample_scan",
    )(page_table.reshape(-1), *([ck] * PAGES_PER_STEP), q_t, knew_t)


SEL_PAGES = MOBA_TOPK * PAGES_PER_BLOCK


ATTEND_HEADS_PER_STEP = 4


def _sample_attend_kernel(pt_ref, ix_ref, slope_ref, s_ref, q_ref, knew_ref, vnew_ref, *refs, past_len):
    n_vt = ATTEND_HEADS_PER_STEP * SEL_PAGES
    vt_refs, o_ref = refs[:n_vt], refs[n_vt]
    n = pl.program_id(0)
    scale = HEAD_DIM ** -0.5
    off = lax.broadcasted_iota(jnp.int32, (1, PAGE_SIZE), 1).astype(F32)
    for hl in range(ATTEND_HEADS_PER_STEP):
        h = pl.program_id(1) * ATTEND_HEADS_PER_STEP + hl
        slope = slope_ref[h]
        qh = q_ref[0, pl.ds(h, 1), :]
        s0 = jnp.sum(qh * knew_ref[0, pl.ds(h, 1), :], axis=-1, keepdims=True) * scale
        rows = []
        for s in range(MOBA_TOPK):
            blk = ix_ref[(n * MOBA_TOPK + s) * N_HEADS + h]
            for half in range(PAGES_PER_BLOCK):
                raw = s_ref[0, blk * PAGES_PER_BLOCK + half, pl.ds(h, 1), :]
                dist = (past_len - blk * MOBA_BLOCK - half * PAGE_SIZE).astype(F32) - off
                rows.append(raw * scale - slope * dist)
        sc = jnp.concatenate(rows, axis=1)
        m = jnp.maximum(s0, jnp.max(sc, axis=-1, keepdims=True))
        p = jnp.exp(sc - m)
        p0 = jnp.exp(s0 - m)
        l = p0 + jnp.sum(p, axis=-1, keepdims=True)
        vt = jnp.concatenate([vt_refs[hl * SEL_PAGES + w][0, 0] for w in range(SEL_PAGES)], axis=1)
        pv = _dot_nt(jnp.broadcast_to(p, (SUBLANES, p.shape[1])).astype(BF16), vt.astype(BF16))[0:1]
        o_ref[0, pl.ds(h, 1), :] = (p0 * vnew_ref[0, pl.ds(h, 1), :] + pv) / l


def _sample_attend(page_table, idx, slopes, s_all, q3, knew3, vnew3, cv, past_len):
    n = q3.shape[0]
    n_pages = page_table.shape[1]

    def vt_spec(hl, w):
        def index(b, g, pt, ix, sl):
            h = g * ATTEND_HEADS_PER_STEP + hl
            blk = ix[(b * MOBA_TOPK + w // PAGES_PER_BLOCK) * N_HEADS + h]
            return (pt[b * n_pages + blk * PAGES_PER_BLOCK + w % PAGES_PER_BLOCK], h, 0, 0)
        return pl.BlockSpec((1, 1, HEAD_DIM, PAGE_SIZE), index)

    tok = pl.BlockSpec((1, N_HEADS, HEAD_DIM), lambda b, g, pt, ix, sl: (b, 0, 0))
    n_vt = ATTEND_HEADS_PER_STEP * SEL_PAGES
    return pl.pallas_call(
        functools.partial(_sample_attend_kernel, past_len=past_len),
        grid_spec=pltpu.PrefetchScalarGridSpec(
            num_scalar_prefetch=3,
            grid=(n, N_HEADS // ATTEND_HEADS_PER_STEP),
            in_specs=[pl.BlockSpec((1,) + s_all.shape[1:], lambda b, g, pt, ix, sl: (b, 0, 0, 0)), tok, tok, tok]
                     + [vt_spec(hl, w) for hl in range(ATTEND_HEADS_PER_STEP) for w in range(SEL_PAGES)],
            out_specs=tok,
        ),
        out_shape=jax.ShapeDtypeStruct((n, N_HEADS, HEAD_DIM), F32),
        compiler_params=_cparams("arbitrary", "arbitrary"),
        name="sample_attend",
    )(page_table.reshape(-1), idx.reshape(-1), slopes, s_all, q3, knew3, vnew3, *([cv] * n_vt))


def _merge_kernel(x_ref, ys_ref, ya_ref, gs_ref, ga_ref, wa_ref, wb_ref, wo_ref, o_ref):
    a = _dot(ys_ref[...].astype(BF16), wa_ref[...])
    b = _dot(ya_ref[...].astype(BF16), wb_ref[...])
    mix = jax.nn.sigmoid(gs_ref[...]) * a + jax.nn.sigmoid(ga_ref[...]) * b
    o_ref[...] = x_ref[...] + _dot(mix.astype(BF16), wo_ref[...])


def _merge(x2d, ys, ya, gs, ga, wa_b, wb_b, wo_b, tm):
    m = x2d.shape[0]
    row = lambda w: pl.BlockSpec((tm, w), lambda i: (i, 0))
    full = lambda a: pl.BlockSpec(a.shape, lambda i: (0,) * a.ndim)
    return pl.pallas_call(
        _merge_kernel,
        grid=(m // tm,),
        in_specs=[row(D_MODEL), row(SSM_WIDTH), row(ATT_WIDTH), row(D_MODEL), row(D_MODEL),
                  full(wa_b), full(wb_b), full(wo_b)],
        out_specs=row(D_MODEL),
        out_shape=jax.ShapeDtypeStruct((m, D_MODEL), F32),
        compiler_params=_cparams("arbitrary"),
        name="merge",
    )(x2d, ys, ya, gs, ga, wa_b, wb_b, wo_b)


FF_CHUNK = D_FF // 2


def _ffn_kernel(x_ref, *refs, tm, sequential):
    if sequential:
        (n2_ref, wup_ref, wgate_ref, cw_ref, cb_ref, wdown_ref, o_ref, cs_ref, tail) = refs
    else:
        (buf0_ref, buf1_ref, n2_ref, wup_ref, wgate_ref, cw_ref, cb_ref, wdown_ref, o_ref, cs0_ref, cs1_ref) = refs
    x = x_ref[...]
    ms = jnp.mean(x * x, axis=-1, keepdims=True)
    h = (x * lax.rsqrt(ms + NORM_EPS) * n2_ref[...]).astype(BF16)

    if sequential:
        @pl.when(pl.program_id(1) == 0)
        def _():
            tail[...] = jnp.zeros_like(tail)
        rows = lax.broadcasted_iota(jnp.int32, (tm, 1), 0)

    y = x
    for c in range(D_FF // FF_CHUNK):
        cs = slice(c * FF_CHUNK, (c + 1) * FF_CHUNK)
        up = _dot(h, wup_ref[:, cs])
        g = _dot(h, wgate_ref[:, cs])
        if sequential:
            p2 = tail[SUBLANES - 2:SUBLANES - 1, cs]
            p1 = tail[SUBLANES - 1:SUBLANES, cs]
            g1 = jnp.where(rows == 0, p1, pltpu.roll(g, 1, 0))
            g2 = jnp.where(rows == 0, p2, jnp.where(rows == 1, p1, pltpu.roll(g, 2, 0)))
            tail[:, cs] = g[tm - SUBLANES:, :]
            cs_ref[0, :, cs] = g[tm - 2:, :]
        else:
            g2 = buf0_ref[:, cs]
            g1 = buf1_ref[:, cs]
            cs0_ref[:, cs] = g1
            cs1_ref[:, cs] = g
        conv = cb_ref[:, cs] + cw_ref[0:1, cs] * g2 + cw_ref[1:2, cs] * g1 + cw_ref[2:3, cs] * g
        act = (jax.nn.gelu(conv) * up).astype(BF16)
        y = y + _dot(act, wdown_ref[cs, :])
    o_ref[...] = y


def _ffn_prompt(x2d, n2, wup_b, wgate_b, cw, cb, wdown_b, n_seq, t_len, tm):
    nt = t_len // tm
    full = lambda a: pl.BlockSpec(a.shape, lambda b, t: (0,) * a.ndim)
    row = pl.BlockSpec((tm, D_MODEL), lambda b, t: (b * nt + t, 0))
    return pl.pallas_call(
        functools.partial(_ffn_kernel, tm=tm, sequential=True),
        grid=(n_seq, nt),
        in_specs=[row, full(n2), full(wup_b), full(wgate_b), full(cw), full(cb), full(wdown_b)],
        out_specs=[row, pl.BlockSpec((1, 2, D_FF), lambda b, t: (b, 0, 0))],
        out_shape=[jax.ShapeDtypeStruct((n_seq * t_len, D_MODEL), F32),
                   jax.ShapeDtypeStruct((n_seq, 2, D_FF), F32)],
        scratch_shapes=[pltpu.VMEM((SUBLANES, D_FF), F32)],
        compiler_params=_cparams("arbitrary", "arbitrary"),
        name="ffn_prompt",
    )(x2d, n2, wup_b, wgate_b, cw, cb, wdown_b)


def _ffn_step(x2d, buf0, buf1, n2, wup_b, wgate_b, cw, cb, wdown_b):
    m = x2d.shape[0]
    args = (x2d, buf0, buf1, n2, wup_b, wgate_b, cw, cb, wdown_b)
    full = lambda a: pl.BlockSpec(a.shape, lambda i: (0,) * a.ndim)
    return pl.pallas_call(
        functools.partial(_ffn_kernel, tm=m, sequential=False),
        grid=(1,),
        in_specs=[full(a) for a in args],
        out_specs=[pl.BlockSpec((m, D_MODEL), lambda i: (0, 0)),
                   pl.BlockSpec((m, D_FF), lambda i: (0, 0)),
                   pl.BlockSpec((m, D_FF), lambda i: (0, 0))],
        out_shape=[jax.ShapeDtypeStruct((m, D_MODEL), F32),
                   jax.ShapeDtypeStruct((m, D_FF), F32),
                   jax.ShapeDtypeStruct((m, D_FF), F32)],
        compiler_params=_cparams("arbitrary"),
        name="ffn_step",
    )(*args)


def _s5_params(lam_re, lam_im, log_dt, b_re, b_im, c_re, c_im):
    lr, li = lam_re.astype(F32), lam_im.astype(F32)
    dt = jnp.exp(log_dt.astype(F32))[:, None]
    mag = jnp.exp(lr * dt)
    ar, ai = mag * jnp.cos(li * dt), mag * jnp.sin(li * dt)
    den = lr * lr + li * li
    fr = ((ar - 1.0) * lr + ai * li) / den
    fi = (ai * lr - (ar - 1.0) * li) / den
    br_, bi_ = b_re.astype(F32), b_im.astype(F32)
    bbr = fr[..., None] * br_ - fi[..., None] * bi_
    bbi = fr[..., None] * bi_ + fi[..., None] * br_
    eye = jnp.eye(SSM_GROUPS, dtype=F32)

    def bdiag_in(w):
        return jnp.einsum('gsc,gh->gchs', w, eye).reshape(SSM_WIDTH, N_STATE)

    def bdiag_out(w):
        return jnp.einsum('gcs,gh->gshc', w, eye).reshape(N_STATE, SSM_WIDTH)

    b_re_d, b_im_d = bdiag_in(bbr), bdiag_in(bbi)
    c_re_d, c_im_d = bdiag_out(c_re.astype(F32)), -bdiag_out(c_im.astype(F32))
    bblk = jnp.concatenate([b_re_d, b_im_d], axis=1).astype(BF16)
    cblk = jnp.concatenate([c_re_d, c_im_d], axis=0).astype(BF16)
    n_in = S5_LANE_CHUNK // SSM_STATE * SSM_GROUP
    chunks = range(N_STATE // S5_LANE_CHUNK)
    cut = lambda m, c: m[c * n_in:(c + 1) * n_in, c * S5_LANE_CHUNK:(c + 1) * S5_LANE_CHUNK]
    cut_t = lambda m, c: m[c * S5_LANE_CHUNK:(c + 1) * S5_LANE_CHUNK, c * n_in:(c + 1) * n_in]
    bb4 = jnp.stack([jnp.concatenate([cut(b_re_d, c), cut(b_im_d, c)], axis=1) for c in chunks]).astype(BF16)
    cc4 = jnp.stack([jnp.concatenate([cut_t(c_re_d, c), cut_t(c_im_d, c)], axis=0) for c in chunks]).astype(BF16)

    a_re, a_im = ar.reshape(1, N_STATE), ai.reshape(1, N_STATE)

    def cmul(x, y):
        return x[0] * y[0] - x[1] * y[1], x[0] * y[1] + x[1] * y[0]

    pows = [(a_re, a_im)]
    for _ in range(SUBLANES - 1):
        pows.append(cmul(pows[-1], (a_re, a_im)))
    rows = jnp.arange(SUBLANES)[:, None]
    tabs = []
    for s in (1, 2, 4):
        tabs.append(jnp.where(rows >= s, pows[s - 1][0], 0.0))
        tabs.append(jnp.where(rows >= s, pows[s - 1][1], 0.0))
    tabs.append(jnp.concatenate([p[0] for p in pows], axis=0))
    tabs.append(jnp.concatenate([p[1] for p in pows], axis=0))
    return a_re, a_im, bblk, cblk, bb4, cc4, jnp.stack(tabs)


def kernel(x_prompt, x_sample, cache_k, cache_v, state_ssm_re, state_ssm_im, state_conv, page_table,
           norm1_w, w_in, q_norm_w, k_norm_w, lam_re, lam_im, log_dt, b_re, b_im, c_re, c_im, d_skip,
           w_glu, b_glu, w_a, w_b, w_o, norm2_w, w_up, w_gate, conv_w, conv_b, w_down):
    depth = norm1_w.shape[0]
    batch, seq, _ = x_prompt.shape
    dec_batch, dec_seq, _ = x_sample.shape
    assert depth == 1 and dec_seq == 1 and seq % MOBA_BLOCK == 0 and dec_batch <= LANES
    past_len = page_table.shape[1] * PAGE_SIZE
    slopes = 2.0 ** (-8.0 * jnp.arange(1, N_HEADS + 1, dtype=F32) / N_HEADS)
    seg = jnp.kron(jnp.eye(N_HEADS, dtype=F32), jnp.ones((HEAD_DIM, HEAD_DIM), F32)).astype(BF16)

    l = 0
    row = lambda a: a.reshape(1, -1).astype(F32)
    n1, n2 = row(norm1_w[l]), row(norm2_w[l])
    qn_row = row(jnp.tile(q_norm_w[l], N_HEADS))
    kn_row = row(jnp.tile(k_norm_w[l], N_HEADS))
    win_b = w_in[l].astype(BF16)
    a_re, a_im, bblk, cblk, bb4, cc4, tabs = _s5_params(lam_re[l], lam_im[l], log_dt[l], b_re[l], b_im[l], c_re[l], c_im[l])
    dskip, bglu = row(d_skip[l]), row(b_glu[l])
    wglu_b = w_glu[l].astype(BF16)
    wa_b, wb_b, wo_b = w_a[l].astype(BF16), w_b[l].astype(BF16), w_o[l].astype(BF16)
    wup_b, wgate_b, wdown_b = w_up[l].astype(BF16), w_gate[l].astype(BF16), w_down[l].astype(BF16)
    cw, cb = conv_w[l].astype(F32), row(conv_b[l])

    mp = batch * seq
    xp = x_prompt.reshape(mp, D_MODEL)
    u, q, gs, ga, k_t, v_t, kb, vta, kmean = _inproj(xp, n1, win_b, qn_row, kn_row, seg, batch, seq, 512, True)
    y_ssm, hre_p, him_p = _s5_prompt(u, bb4, cc4, tabs, dskip, wglu_b, bglu, batch, seq, 256)
    y_att = _moba_prompt(q.reshape(batch, seq, ATT_WIDTH), kb.reshape(batch, seq, ATT_WIDTH), vta,
                         kmean.reshape(batch, seq // MOBA_BLOCK, ATT_WIDTH), slopes)
    x1 = _merge(xp, y_ssm, y_att.reshape(mp, ATT_WIDTH), gs, ga, wa_b, wb_b, wo_b, 512)
    yp, conv_p = _ffn_prompt(x1, n2, wup_b, wgate_b, cw, cb, wdown_b, batch, seq, 512)

    xs = x_sample.reshape(dec_batch, D_MODEL)
    u_s, q_s, gs_s, ga_s, k_s, v_s = _inproj(xs, n1, win_b, qn_row, kn_row, seg, 1, dec_batch, dec_batch, False)
    y_ssm_s, hre_s, him_s = _s5_step(u_s, state_ssm_re[l].reshape(dec_batch, N_STATE),
                                     state_ssm_im[l].reshape(dec_batch, N_STATE),
                                     a_re, a_im, bblk, cblk, dskip, wglu_b, bglu)
    heads = lambda a: a.reshape(dec_batch, N_HEADS, HEAD_DIM)
    pad = lambda a_t: jnp.pad(a_t, ((0, 0), (0, LANES - dec_batch)))
    ck = jnp.transpose(cache_k[l], (0, 2, 3, 1))
    cv = jnp.transpose(cache_v[l], (0, 2, 3, 1))
    s_all, idx = _sample_scan(page_table, ck, pad(q_s.T), pad(k_s.T))
    y_att_s = _sample_attend(page_table, idx[..., 0], slopes, s_all, heads(q_s), heads(k_s), heads(v_s), cv, past_len)
    x1_s = _merge(xs, y_ssm_s, y_att_s.reshape(dec_batch, ATT_WIDTH), gs_s, ga_s, wa_b, wb_b, wo_b, dec_batch)
    ys, cs0, cs1 = _ffn_step(x1_s, state_conv[l, :, 0, :], state_conv[l, :, 1, :], n2, wup_b, wgate_b, cw, cb, wdown_b)

    g, s = SSM_GROUPS, SSM_STATE
    kv_out = lambda a_t: jnp.transpose(a_t.reshape(batch, N_HEADS, HEAD_DIM, seq), (0, 3, 1, 2))[None]
    return (yp.reshape(batch, seq, D_MODEL),
            ys.reshape(dec_batch, 1, D_MODEL),
            kv_out(k_t),
            kv_out(v_t),
            hre_p.reshape(1, batch, g, s),
            him_p.reshape(1, batch, g, s),
            conv_p.reshape(1, batch, 2, D_FF),
            k_s.reshape(1, dec_batch, 1, N_HEADS, HEAD_DIM),
            v_s.reshape(1, dec_batch, 1, N_HEADS, HEAD_DIM),
            hre_s.reshape(1, dec_batch, g, s),
            him_s.reshape(1, dec_batch, g, s),
            jnp.stack([cs0, cs1], axis=1).reshape(1, dec_batch, 2, D_FF))
```

```python
import functools

import jax
import jax.numpy as jnp
from jax import lax
from jax.experimental import pallas as pl
from jax.experimental.pallas import tpu as pltpu

F32 = jnp.float32
BF16 = jnp.bfloat16

D_MODEL = 1024
SSM_WIDTH = 512
SSM_GROUP = 16
SSM_GROUPS = 32
SSM_STATE = 64
N_STATE = SSM_GROUPS * SSM_STATE
N_HEADS = 8
HEAD_DIM = 64
ATT_WIDTH = 512
MOBA_BLOCK = 256
MOBA_TOPK = 3
D_FF = 2816
NORM_EPS = 1e-6
PAGE_SIZE = 128
LOG2E = 1.4426950408889634
NEG = -1e30

VMEM_LIMIT_BYTES = 56 * 1024 * 1024
SUBLANES = 8
LANES = 128


def _cparams(*sem):
    return pltpu.CompilerParams(dimension_semantics=sem, vmem_limit_bytes=VMEM_LIMIT_BYTES)


def _dot(a, b):
    return jnp.dot(a, b, preferred_element_type=F32)


def _dot_nt(a, b):
    return lax.dot_general(a, b, (((1,), (1,)), ((), ())), preferred_element_type=F32)


def _split_bf16(a):
    hi = a.astype(BF16)
    lo = (a - hi.astype(F32)).astype(BF16)
    return hi, lo


V_ROWS = 80


def _inproj_kernel(x_ref, n1_ref, win_ref, qn_ref, kn_ref, seg_ref, u_ref, q_ref, gs_ref, ga_ref, *rest,
                   tm, prompt):
    x = x_ref[...]
    ms = jnp.mean(x * x, axis=-1, keepdims=True)
    h = (x * lax.rsqrt(ms + NORM_EPS) * n1_ref[...]).astype(BF16)

    def proj(lo, hi):
        return _dot(h, win_ref[:, lo:hi])

    def head_rmsnorm(z, w_row):
        sq_hi, sq_lo = _split_bf16(z * z)
        ss = _dot(sq_hi, seg_ref[...]) + _dot(sq_lo, seg_ref[...])
        return z * lax.rsqrt(ss * (1.0 / HEAD_DIM) + NORM_EPS) * w_row

    o1 = SSM_WIDTH
    o2 = o1 + ATT_WIDTH
    o3 = o2 + ATT_WIDTH
    o4 = o3 + ATT_WIDTH
    o5 = o4 + D_MODEL
    u_ref[...] = proj(0, o1)
    q_ref[...] = head_rmsnorm(proj(o1, o2), qn_ref[...])
    k = head_rmsnorm(proj(o2, o3), kn_ref[...])
    v = proj(o3, o4)
    gs_ref[...] = proj(o4, o5)
    ga_ref[...] = proj(o5, o5 + D_MODEL)
    if not prompt:
        k_ref, v_ref = rest
        k_ref[...] = k
        v_ref[...] = v
        return
    kt_ref, vt_ref, kb_ref, vta_ref, kmean_ref = rest
    nb = tm // MOBA_BLOCK
    kt_ref[0] = k.T
    vt = v.T
    vt_ref[0] = vt
    kb_ref[...] = k.astype(BF16)
    ones = jnp.ones((V_ROWS - HEAD_DIM, MOBA_BLOCK), BF16)
    for hd in range(N_HEADS):
        for b in range(nb):
            vta_ref[0, b, hd * V_ROWS:hd * V_ROWS + HEAD_DIM, :] = (
                vt[hd * HEAD_DIM:(hd + 1) * HEAD_DIM, b * MOBA_BLOCK:(b + 1) * MOBA_BLOCK].astype(BF16))
            vta_ref[0, b, hd * V_ROWS + HEAD_DIM:(hd + 1) * V_ROWS, :] = ones
    ksum = k.reshape(nb, MOBA_BLOCK, ATT_WIDTH).sum(axis=1)
    kmean_ref[...] = (ksum * (1.0 / MOBA_BLOCK)).reshape(nb, 1, ATT_WIDTH)


def _inproj(x2d, n1, win_b, qn_row, kn_row, seg, n_seq, t_len, tm, prompt):
    nt = t_len // tm
    m = n_seq * t_len
    row = lambda w: pl.BlockSpec((tm, w), lambda b, t: (b * nt + t, 0))
    full = lambda a: pl.BlockSpec(a.shape, lambda b, t: (0,) * a.ndim)
    f32 = lambda *s: jax.ShapeDtypeStruct(s, F32)
    out_shape = [f32(m, SSM_WIDTH), f32(m, ATT_WIDTH), f32(m, D_MODEL), f32(m, D_MODEL)]
    out_specs = [row(SSM_WIDTH), row(ATT_WIDTH), row(D_MODEL), row(D_MODEL)]
    if prompt:
        nb = tm // MOBA_BLOCK
        tspec = pl.BlockSpec((1, ATT_WIDTH, tm), lambda b, t: (b, 0, t))
        out_shape += [f32(n_seq, ATT_WIDTH, t_len), f32(n_seq, ATT_WIDTH, t_len),
                      jax.ShapeDtypeStruct((m, ATT_WIDTH), BF16),
                      jax.ShapeDtypeStruct((n_seq, t_len // MOBA_BLOCK, N_HEADS * V_ROWS, MOBA_BLOCK), BF16),
                      f32(m // MOBA_BLOCK, 1, ATT_WIDTH)]
        out_specs += [tspec, tspec, row(ATT_WIDTH),
                      pl.BlockSpec((1, nb, N_HEADS * V_ROWS, MOBA_BLOCK), lambda b, t: (b, t, 0, 0)),
                      pl.BlockSpec((nb, 1, ATT_WIDTH), lambda b, t: (b * nt + t, 0, 0))]
    else:
        out_shape += [f32(m, ATT_WIDTH), f32(m, ATT_WIDTH)]
        out_specs += [row(ATT_WIDTH), row(ATT_WIDTH)]
    return pl.pallas_call(
        functools.partial(_inproj_kernel, tm=tm, prompt=prompt),
        grid=(n_seq, nt),
        in_specs=[row(D_MODEL), full(n1), full(win_b), full(qn_row), full(kn_row), full(seg)],
        out_specs=out_specs,
        out_shape=out_shape,
        compiler_params=_cparams("arbitrary", "arbitrary"),
        name="inproj",
    )(x2d, n1, win_b, qn_row, kn_row, seg)


def _s5_output(y_lin, u, dskip_ref, wglu_ref, bglu_ref):
    y = jax.nn.gelu(y_lin + dskip_ref[...] * u)
    gate = jax.nn.sigmoid(_dot(y.astype(BF16), wglu_ref[...]) + bglu_ref[...])
    return y * gate


S5_LANE_CHUNK = 512
S5_MXU_LAGS = 4


def _s5_prompt_kernel(u_ref, bb_ref, cc_ref, tabs_ref, dskip_ref, wglu_ref, bglu_ref,
                      y_ref, hre_ref, him_ref, xre, xim, cre, cim, *, tt):
    t = pl.program_id(1)

    @pl.when(t == 0)
    def _():
        cre[...] = jnp.zeros_like(cre)
        cim[...] = jnp.zeros_like(cim)

    u = u_ref[...]
    n_in = S5_LANE_CHUNK // SSM_STATE * SSM_GROUP
    lag_row = lax.broadcasted_iota(jnp.int32, (tt, 1), 0) % S5_MXU_LAGS
    y_parts = []
    for c in range(N_STATE // S5_LANE_CHUNK):
        ls = slice(c * S5_LANE_CHUNK, (c + 1) * S5_LANE_CHUNK)
        uc = u[:, c * n_in:(c + 1) * n_in]
        lagged = [uc.astype(BF16)]
        for lag in range(1, S5_MXU_LAGS):
            lagged.append(jnp.where(lag_row >= lag, pltpu.roll(uc, lag, 0), 0.0).astype(BF16))
        ustack = jnp.concatenate(lagged, axis=1)
        xre[:, ls] = _dot(ustack, bb_ref[c, :, :S5_LANE_CHUNK])
        xim[:, ls] = _dot(ustack, bb_ref[c, :, S5_LANE_CHUNK:])
        cr, ci = cre[:, ls], cim[:, ls]
        for b in range(tt // SUBLANES):
            rs = slice(b * SUBLANES, (b + 1) * SUBLANES)
            xr, xi = xre[rs, ls], xim[rs, ls]
            shape = (SUBLANES, S5_LANE_CHUNK)
            last = S5_MXU_LAGS - 1
            rr = jnp.broadcast_to(xr[last:last + 1, :], shape)
            ri = jnp.broadcast_to(xi[last:last + 1, :], shape)
            pr, pi = tabs_ref[0, :, ls], tabs_ref[1, :, ls]
            xr, xi = xr + (pr * rr - pi * ri), xi + (pr * ri + pi * rr)
            pr, pi = tabs_ref[2, :, ls], tabs_ref[3, :, ls]
            hr = xr + (pr * cr - pi * ci)
            hi = xi + (pr * ci + pi * cr)
            xre[rs, ls] = hr
            xim[rs, ls] = hi
            shape = (SUBLANES, S5_LANE_CHUNK)
            cr = jnp.broadcast_to(hr[SUBLANES - 1:SUBLANES, :], shape)
            ci = jnp.broadcast_to(hi[SUBLANES - 1:SUBLANES, :], shape)
        cre[:, ls] = cr
        cim[:, ls] = ci
        y_parts.append(_dot(xre[:, ls].astype(BF16), cc_ref[c, :S5_LANE_CHUNK, :])
                       + _dot(xim[:, ls].astype(BF16), cc_ref[c, S5_LANE_CHUNK:, :]))

    hre_ref[0] = cre[0:1, :]
    him_ref[0] = cim[0:1, :]
    y_ref[...] = _s5_output(jnp.concatenate(y_parts, axis=1), u, dskip_ref, wglu_ref, bglu_ref)


def _s5_prompt(u2d, bblk, cblk, tabs, dskip, wglu_b, bglu, n_seq, t_len, tt):
    nt = t_len // tt
    full = lambda a: pl.BlockSpec(a.shape, lambda b, t: (0,) * a.ndim)
    return pl.pallas_call(
        functools.partial(_s5_prompt_kernel, tt=tt),
        grid=(n_seq, nt),
        in_specs=[pl.BlockSpec((tt, SSM_WIDTH), lambda b, t: (b * nt + t, 0)),
                  full(bblk), full(cblk), full(tabs), full(dskip), full(wglu_b), full(bglu)],
        out_specs=[pl.BlockSpec((tt, SSM_WIDTH), lambda b, t: (b * nt + t, 0)),
                   pl.BlockSpec((1, 1, N_STATE), lambda b, t: (b, 0, 0)),
                   pl.BlockSpec((1, 1, N_STATE), lambda b, t: (b, 0, 0))],
        out_shape=[jax.ShapeDtypeStruct((n_seq * t_len, SSM_WIDTH), F32),
                   jax.ShapeDtypeStruct((n_seq, 1, N_STATE), F32),
                   jax.ShapeDtypeStruct((n_seq, 1, N_STATE), F32)],
        scratch_shapes=[pltpu.VMEM((tt, N_STATE), F32), pltpu.VMEM((tt, N_STATE), F32),
                        pltpu.VMEM((SUBLANES, N_STATE), F32), pltpu.VMEM((SUBLANES, N_STATE), F32)],
        compiler_params=_cparams("arbitrary", "arbitrary"),
        name="s5_prompt",
    )(u2d, bblk, cblk, tabs, dskip, wglu_b, bglu)


def _s5_step_kernel(u_ref, h0re_ref, h0im_ref, are_ref, aim_ref, bblk_ref, cblk_ref,
                    dskip_ref, wglu_ref, bglu_ref, y_ref, hre_ref, him_ref):
    u = u_ref[...]
    ub = u.astype(BF16)
    ar = are_ref[...]
    ai = aim_ref[...]
    h0r = h0re_ref[...]
    h0i = h0im_ref[...]
    hr = _dot(ub, bblk_ref[:, :N_STATE]) + (ar * h0r - ai * h0i)
    hi = _dot(ub, bblk_ref[:, N_STATE:]) + (ar * h0i + ai * h0r)
    hre_ref[...] = hr
    him_ref[...] = hi
    y_lin = _dot(hr.astype(BF16), cblk_ref[:N_STATE, :]) + _dot(hi.astype(BF16), cblk_ref[N_STATE:, :])
    y_ref[...] = _s5_output(y_lin, u, dskip_ref, wglu_ref, bglu_ref)


def _s5_step(u2d, h0re, h0im, are, aim, bblk, cblk, dskip, wglu_b, bglu):
    m = u2d.shape[0]
    args = (u2d, h0re, h0im, are, aim, bblk, cblk, dskip, wglu_b, bglu)
    full = lambda a: pl.BlockSpec(a.shape, lambda i: (0,) * a.ndim)
    return pl.pallas_call(
        _s5_step_kernel,
        grid=(1,),
        in_specs=[full(a) for a in args],
        out_specs=[pl.BlockSpec((m, SSM_WIDTH), lambda i: (0, 0)),
                   pl.BlockSpec((m, N_STATE), lambda i: (0, 0)),
                   pl.BlockSpec((m, N_STATE), lambda i: (0, 0))],
        out_shape=[jax.ShapeDtypeStruct((m, SSM_WIDTH), F32),
                   jax.ShapeDtypeStruct((m, N_STATE), F32),
                   jax.ShapeDtypeStruct((m, N_STATE), F32)],
        compiler_params=_cparams("arbitrary"),
        name="s5_step",
    )(*args)


def _topk_bias_t(gate_t, nblk):
    bidx = lax.broadcasted_iota(jnp.int32, gate_t.shape, 0).astype(F32)
    bias = jnp.full(gate_t.shape, NEG, F32)
    g = gate_t
    for _ in range(MOBA_TOPK):
        mx = jnp.max(g, axis=0, keepdims=True)
        am = jnp.min(jnp.where(g == mx, bidx, float(nblk)), axis=0, keepdims=True)
        hit = bidx == am
        bias = jnp.where(hit & (mx > -jnp.inf), 0.0, bias)
        g = jnp.where(hit, -jnp.inf, g)
    return bias


ALIBI_PARTS = 3
SCORE_LOOKAHEAD = 6
PAST_BLOCKS_PER_TRIP = 4


def _moba_prompt_kernel(q_ref, kb_ref, vta_ref, kmean_ref, slope_ref, o_ref,
                        qt_sc, sel_sc, m_sc, acc_sc, ak_sc, *, nblk):
    i = pl.program_id(1)
    bq = MOBA_BLOCK
    key_idx = lax.broadcasted_iota(jnp.int32, (bq, bq), 0)
    qry_idx = lax.broadcasted_iota(jnp.int32, (bq, bq), 1)
    lane = lax.broadcasted_iota(jnp.int32, (1, LANES), 1)
    sub = lax.broadcasted_iota(jnp.int32, (LANES, 1), 0)
    gblk = lax.broadcasted_iota(jnp.int32, (nblk, bq), 0)


    @pl.when((pl.program_id(0) == 0) & (i == 0))
    def _():
        key_pos = lax.broadcasted_iota(jnp.int32, (bq, LANES), 0).astype(F32)
        lane_b = lax.broadcasted_iota(jnp.int32, (bq, LANES), 1)
        for p in range(N_HEADS // 2):
            slope = jnp.where(lane_b // ALIBI_PARTS == 0, slope_ref[2 * p], slope_ref[2 * p + 1]) * LOG2E
            rest = slope * key_pos
            tile = jnp.zeros((bq, LANES), F32)
            for k in range(ALIBI_PARTS):
                part = rest.astype(BF16).astype(F32)
                tile = jnp.where(lane_b % ALIBI_PARTS == k, part, tile)
                rest = rest - part
            ak_sc[p] = jnp.where(lane_b < 2 * ALIBI_PARTS, tile, 0.0).astype(BF16)
        for h in range(N_HEADS):
            qt_sc[h, LANES:, :] = jnp.broadcast_to(
                jnp.where((sub // ALIBI_PARTS) == h % 2, 1.0, 0.0).astype(BF16), (LANES, bq))

    for p in range(N_HEADS // 2):
        ls = slice(p * LANES, (p + 1) * LANES)
        qp = q_ref[0, :, ls]
        q_hi, q_lo = _split_bf16(qp)
        qt = qp.T * (HEAD_DIM ** -0.5 * LOG2E)
        km = kmean_ref[0, :, ls]
        for hh in range(2):
            h = 2 * p + hh
            qt_sc[h, :LANES, :] = jnp.where((sub // HEAD_DIM) == hh, qt, 0.0).astype(BF16)
            km_hi, km_lo = _split_bf16(jnp.where((lane // HEAD_DIM) == hh, km, 0.0))
            gate_t = _dot_nt(km_hi, q_hi) + _dot_nt(km_hi, q_lo) + _dot_nt(km_lo, q_hi)
            sel_sc[h] = _topk_bias_t(jnp.where(gblk < i, gate_t, -jnp.inf), nblk)

    def scores(j, h):
        c0 = pl.multiple_of(j * bq, bq)
        kj = kb_ref[0, pl.ds(c0, bq), (h // 2) * LANES:(h // 2 + 1) * LANES]
        return _dot(jnp.concatenate([kj, ak_sc[h // 2]], axis=1), qt_sc[h])

    def values(j, h):
        return vta_ref[0, j, h * V_ROWS:(h + 1) * V_ROWS, :]

    def pipelined(blocks, update_fn):
        tasks = [(j, h) for j in blocks for h in range(N_HEADS)]
        pending = [scores(*t) for t in tasks[:SCORE_LOOKAHEAD]]
        for k, (j, h) in enumerate(tasks):
            if k + SCORE_LOOKAHEAD < len(tasks):
                pending.append(scores(*tasks[k + SCORE_LOOKAHEAD]))
            update_fn(j, h, pending.pop(0))

    def own_update(j, h, s_t):
        s_t = jnp.where(key_idx <= qry_idx, s_t, NEG)
        m = jnp.max(s_t, axis=0, keepdims=True)
        m_sc[h] = jnp.broadcast_to(m, (SUBLANES, bq))
        acc_sc[h] = _dot(values(j, h), jnp.exp2(s_t - m).astype(BF16))

    pipelined([i], own_update)

    def past_update(j, h, s_t):
        bias = sel_sc[h, pl.ds(j, 1), :] + (slope_ref[h] * LOG2E) * ((j - i) * bq).astype(F32)
        m_old = m_sc[h][0:1, :]
        m_new = jnp.maximum(m_old, jnp.max(s_t, axis=0, keepdims=True) + bias)
        m_sc[h] = jnp.broadcast_to(m_new, (SUBLANES, bq))
        pv = _dot(values(j, h), jnp.exp2(s_t - (m_new - bias)).astype(BF16))
        acc_sc[h] = jnp.exp2(m_old - m_new) * acc_sc[h] + pv

    def body(t, carry):
        pipelined([PAST_BLOCKS_PER_TRIP * t + d for d in range(PAST_BLOCKS_PER_TRIP)], past_update)
        return carry

    n_trips = i // PAST_BLOCKS_PER_TRIP
    lax.fori_loop(0, n_trips, body, 0)
    j_left = n_trips * PAST_BLOCKS_PER_TRIP
    width = PAST_BLOCKS_PER_TRIP // 2
    while width:
        @pl.when((i & width) != 0)
        def _(j_left=j_left, width=width):
            pipelined([j_left + d for d in range(width)], past_update)

        j_left = j_left + (i & width)
        width //= 2

    for p in range(N_HEADS // 2):
        halves = []
        for h in (2 * p, 2 * p + 1):
            acc = acc_sc[h]
            halves.append(acc[:HEAD_DIM, :] / acc[HEAD_DIM:HEAD_DIM + 1, :])
        o_ref[0, :, p * LANES:(p + 1) * LANES] = jnp.concatenate(halves, axis=0).T


def _moba_prompt(q3, kb3, vta, kmean3, slopes):
    n, t_len, _ = q3.shape
    nblk = t_len // MOBA_BLOCK
    return pl.pallas_call(
        functools.partial(_moba_prompt_kernel, nblk=nblk),
        grid=(n, nblk),
        in_specs=[pl.BlockSpec((1, MOBA_BLOCK, ATT_WIDTH), lambda b, i: (b, i, 0)),
                  pl.BlockSpec((1, t_len, ATT_WIDTH), lambda b, i: (b, 0, 0)),
                  pl.BlockSpec((1, nblk, N_HEADS * V_ROWS, MOBA_BLOCK), lambda b, i: (b, 0, 0, 0)),
                  pl.BlockSpec((1, nblk, ATT_WIDTH), lambda b, i: (b, 0, 0)),
                  pl.BlockSpec(memory_space=pltpu.SMEM)],
        out_specs=pl.BlockSpec((1, MOBA_BLOCK, ATT_WIDTH), lambda b, i: (b, i, 0)),
        out_shape=jax.ShapeDtypeStruct((n, t_len, ATT_WIDTH), F32),
        scratch_shapes=[pltpu.VMEM((N_HEADS, 2 * LANES, MOBA_BLOCK), BF16),
                        pltpu.VMEM((N_HEADS, nblk, MOBA_BLOCK), F32),
                        pltpu.VMEM((N_HEADS, SUBLANES, MOBA_BLOCK), F32),
                        pltpu.VMEM((N_HEADS, V_ROWS, MOBA_BLOCK), F32),
                        pltpu.VMEM((N_HEADS // 2, MOBA_BLOCK, LANES), BF16)],
        compiler_params=_cparams("arbitrary", "arbitrary"),
        name="moba_prompt",
    )(q3, kb3, vta, kmean3, slopes)


PAGES_PER_STEP = 32
PAGES_PER_BLOCK = MOBA_BLOCK // PAGE_SIZE
BLOCKS_PER_STEP = PAGES_PER_STEP // PAGES_PER_BLOCK


def _sample_scan_kernel(pt_ref, *refs, n_past):
    pages = refs[:PAGES_PER_STEP]
    qt_ref, knt_ref, s_ref, idx_ref, qrep_sc, g_sc = refs[PAGES_PER_STEP:]
    n = pl.program_id(0)
    c = pl.program_id(1)
    lane = lax.broadcasted_iota(jnp.int32, (1, LANES), 1)

    @pl.when(c == 0)
    def _():
        qcol = jnp.sum(jnp.where(lane == n, qt_ref[...], 0.0), axis=-1, keepdims=True)
        qrep_sc[...] = jnp.broadcast_to(qcol, qrep_sc.shape)
        g_sc[...] = jnp.zeros_like(g_sc)

    qrep = qrep_sc[...].reshape(N_HEADS, HEAD_DIM, LANES)
    g = g_sc[...]
    for r in range(0, PAGES_PER_STEP, PAGES_PER_BLOCK):
        blk = jnp.zeros((N_HEADS, LANES), F32)
        for rr in range(r, r + PAGES_PER_BLOCK):
            sc = jnp.sum(pages[rr][0] * qrep, axis=1)
            s_ref[0, rr] = sc
            blk = blk + sc
        g = jnp.where(lane == c * BLOCKS_PER_STEP + r // PAGES_PER_BLOCK,
                      jnp.sum(blk, axis=-1, keepdims=True), g)
    g_sc[...] = g

    @pl.when(c == pl.num_programs(1) - 1)
    def _():
        kcol = jnp.sum(jnp.where(lane == n, knt_ref[...], 0.0), axis=-1, keepdims=True)
        g_own = jnp.sum((qrep_sc[:, 0:1] * kcol).reshape(N_HEADS, HEAD_DIM, 1), axis=1)
        gate = jnp.where(lane == n_past, g_own, g) * (1.0 / MOBA_BLOCK)
        gsel = jnp.where(lane < n_past, gate, -jnp.inf)
        lanef = lane.astype(F32)
        for k in range(MOBA_TOPK):
            mx = jnp.max(gsel, axis=-1, keepdims=True)
            am = jnp.min(jnp.where(gsel == mx, lanef, float(LANES)), axis=-1, keepdims=True)
            idx_ref[0, k] = jnp.broadcast_to(am.astype(jnp.int32), (N_HEADS, LANES))
            gsel = jnp.where(lanef == am, -jnp.inf, gsel)


def _sample_scan(page_table, ck, q_t, knew_t):
    n, n_pages = page_table.shape
    steps = n_pages // PAGES_PER_STEP
    n_past = n_pages // PAGES_PER_BLOCK

    def page_spec(r):
        return pl.BlockSpec((1, N_HEADS, HEAD_DIM, PAGE_SIZE),
                            lambda b, c, pt: (pt[b * n_pages + c * PAGES_PER_STEP + r], 0, 0, 0))

    full = lambda a: pl.BlockSpec(a.shape, lambda b, c, pt: (0,) * a.ndim)
    return pl.pallas_call(
        functools.partial(_sample_scan_kernel, n_past=n_past),
        grid_spec=pltpu.PrefetchScalarGridSpec(
            num_scalar_prefetch=1,
            grid=(n, steps),
            in_specs=[page_spec(r) for r in range(PAGES_PER_STEP)] + [full(q_t), full(knew_t)],
            out_specs=[pl.BlockSpec((1, PAGES_PER_STEP, N_HEADS, PAGE_SIZE), lambda b, c, pt: (b, c, 0, 0)),
                       pl.BlockSpec((1, MOBA_TOPK, N_HEADS, LANES), lambda b, c, pt: (b, 0, 0, 0))],
            scratch_shapes=[pltpu.VMEM((ATT_WIDTH, LANES), F32), pltpu.VMEM((N_HEADS, LANES), F32)],
        ),
        out_shape=[jax.ShapeDtypeStruct((n, n_pages, N_HEADS, PAGE_SIZE), F32),
                   jax.ShapeDtypeStruct((n, MOBA_TOPK, N_HEADS, LANES), jnp.int32)],
        compiler_params=_cparams("arbitrary", "arbitrary"),
        name="sample_scan",
    )(page_table.reshape(-1), *([ck] * PAGES_PER_STEP), q_t, knew_t)


SEL_PAGES = MOBA_TOPK * PAGES_PER_BLOCK


ATTEND_HEADS_PER_STEP = 4


def _sample_attend_kernel(pt_ref, ix_ref, slope_ref, s_ref, q_ref, knew_ref, vnew_ref, *refs, past_len):
    n_vt = ATTEND_HEADS_PER_STEP * SEL_PAGES
    vt_refs, o_ref = refs[:n_vt], refs[n_vt]
    n = pl.program_id(0)
    scale = HEAD_DIM ** -0.5
    off = lax.broadcasted_iota(jnp.int32, (1, PAGE_SIZE), 1).astype(F32)
    for hl in range(ATTEND_HEADS_PER_STEP):
        h = pl.program_id(1) * ATTEND_HEADS_PER_STEP + hl
        slope = slope_ref[h]
        qh = q_ref[0, pl.ds(h, 1), :]
        s0 = jnp.sum(qh * knew_ref[0, pl.ds(h, 1), :], axis=-1, keepdims=True) * scale
        rows = []
        for s in range(MOBA_TOPK):
            blk = ix_ref[(n * MOBA_TOPK + s) * N_HEADS + h]
            for half in range(PAGES_PER_BLOCK):
                raw = s_ref[0, blk * PAGES_PER_BLOCK + half, pl.ds(h, 1), :]
                dist = (past_len - blk * MOBA_BLOCK - half * PAGE_SIZE).astype(F32) - off
                rows.append(raw * scale - slope * dist)
        sc = jnp.concatenate(rows, axis=1)
        m = jnp.maximum(s0, jnp.max(sc, axis=-1, keepdims=True))
        p = jnp.exp(sc - m)
        p0 = jnp.exp(s0 - m)
        l = p0 + jnp.sum(p, axis=-1, keepdims=True)
        vt = jnp.concatenate([vt_refs[hl * SEL_PAGES + w][0, 0] for w in range(SEL_PAGES)], axis=1)
        pv = _dot_nt(jnp.broadcast_to(p, (SUBLANES, p.shape[1])).astype(BF16), vt.astype(BF16))[0:1]
        o_ref[0, pl.ds(h, 1), :] = (p0 * vnew_ref[0, pl.ds(h, 1), :] + pv) / l


def _sample_attend(page_table, idx, slopes, s_all, q3, knew3, vnew3, cv, past_len):
    n = q3.shape[0]
    n_pages = page_table.shape[1]

    def vt_spec(hl, w):
        def index(b, g, pt, ix, sl):
            h = g * ATTEND_HEADS_PER_STEP + hl
            blk = ix[(b * MOBA_TOPK + w // PAGES_PER_BLOCK) * N_HEADS + h]
            return (pt[b * n_pages + blk * PAGES_PER_BLOCK + w % PAGES_PER_BLOCK], h, 0, 0)
        return pl.BlockSpec((1, 1, HEAD_DIM, PAGE_SIZE), index)

    tok = pl.BlockSpec((1, N_HEADS, HEAD_DIM), lambda b, g, pt, ix, sl: (b, 0, 0))
    n_vt = ATTEND_HEADS_PER_STEP * SEL_PAGES
    return pl.pallas_call(
        functools.partial(_sample_attend_kernel, past_len=past_len),
        grid_spec=pltpu.PrefetchScalarGridSpec(
            num_scalar_prefetch=3,
            grid=(n, N_HEADS // ATTEND_HEADS_PER_STEP),
            in_specs=[pl.BlockSpec((1,) + s_all.shape[1:], lambda b, g, pt, ix, sl: (b, 0, 0, 0)), tok, tok, tok]
                     + [vt_spec(hl, w) for hl in range(ATTEND_HEADS_PER_STEP) for w in range(SEL_PAGES)],
            out_specs=tok,
        ),
        out_shape=jax.ShapeDtypeStruct((n, N_HEADS, HEAD_DIM), F32),
        compiler_params=_cparams("arbitrary", "arbitrary"),
        name="sample_attend",
    )(page_table.reshape(-1), idx.reshape(-1), slopes, s_all, q3, knew3, vnew3, *([cv] * n_vt))


def _merge_kernel(x_ref, ys_ref, ya_ref, gs_ref, ga_ref, wa_ref, wb_ref, wo_ref, o_ref):
    a = _dot(ys_ref[...].astype(BF16), wa_ref[...])
    b = _dot(ya_ref[...].astype(BF16), wb_ref[...])
    mix = jax.nn.sigmoid(gs_ref[...]) * a + jax.nn.sigmoid(ga_ref[...]) * b
    o_ref[...] = x_ref[...] + _dot(mix.astype(BF16), wo_ref[...])


def _merge(x2d, ys, ya, gs, ga, wa_b, wb_b, wo_b, tm):
    m = x2d.shape[0]
    row = lambda w: pl.BlockSpec((tm, w), lambda i: (i, 0))
    full = lambda a: pl.BlockSpec(a.shape, lambda i: (0,) * a.ndim)
    return pl.pallas_call(
        _merge_kernel,
        grid=(m // tm,),
        in_specs=[row(D_MODEL), row(SSM_WIDTH), row(ATT_WIDTH), row(D_MODEL), row(D_MODEL),
                  full(wa_b), full(wb_b), full(wo_b)],
        out_specs=row(D_MODEL),
        out_shape=jax.ShapeDtypeStruct((m, D_MODEL), F32),
        compiler_params=_cparams("arbitrary"),
        name="merge",
    )(x2d, ys, ya, gs, ga, wa_b, wb_b, wo_b)


FF_CHUNK = D_FF // 2


def _ffn_kernel(x_ref, *refs, tm, sequential):
    if sequential:
        (n2_ref, wup_ref, wgate_ref, cw_ref, cb_ref, wdown_ref, o_ref, cs_ref, tail) = refs
    else:
        (buf0_ref, buf1_ref, n2_ref, wup_ref, wgate_ref, cw_ref, cb_ref, wdown_ref, o_ref, cs0_ref, cs1_ref) = refs
    x = x_ref[...]
    ms = jnp.mean(x * x, axis=-1, keepdims=True)
    h = (x * lax.rsqrt(ms + NORM_EPS) * n2_ref[...]).astype(BF16)

    if sequential:
        @pl.when(pl.program_id(1) == 0)
        def _():
            tail[...] = jnp.zeros_like(tail)
        rows = lax.broadcasted_iota(jnp.int32, (tm, 1), 0)

    y = x
    for c in range(D_FF // FF_CHUNK):
        cs = slice(c * FF_CHUNK, (c + 1) * FF_CHUNK)
        up = _dot(h, wup_ref[:, cs])
        g = _dot(h, wgate_ref[:, cs])
        if sequential:
            p2 = tail[SUBLANES - 2:SUBLANES - 1, cs]
            p1 = tail[SUBLANES - 1:SUBLANES, cs]
            g1 = jnp.where(rows == 0, p1, pltpu.roll(g, 1, 0))
            g2 = jnp.where(rows == 0, p2, jnp.where(rows == 1, p1, pltpu.roll(g, 2, 0)))
            tail[:, cs] = g[tm - SUBLANES:, :]
            cs_ref[0, :, cs] = g[tm - 2:, :]
        else:
            g2 = buf0_ref[:, cs]
            g1 = buf1_ref[:, cs]
            cs0_ref[:, cs] = g1
            cs1_ref[:, cs] = g
        conv = cb_ref[:, cs] + cw_ref[0:1, cs] * g2 + cw_ref[1:2, cs] * g1 + cw_ref[2:3, cs] * g
        act = (jax.nn.gelu(conv) * up).astype(BF16)
        y = y + _dot(act, wdown_ref[cs, :])
    o_ref[...] = y


def _ffn_prompt(x2d, n2, wup_b, wgate_b, cw, cb, wdown_b, n_seq, t_len, tm):
    nt = t_len // tm
    full = lambda a: pl.BlockSpec(a.shape, lambda b, t: (0,) * a.ndim)
    row = pl.BlockSpec((tm, D_MODEL), lambda b, t: (b * nt + t, 0))
    return pl.pallas_call(
        functools.partial(_ffn_kernel, tm=tm, sequential=True),
        grid=(n_seq, nt),
        in_specs=[row, full(n2), full(wup_b), full(wgate_b), full(cw), full(cb), full(wdown_b)],
        out_specs=[row, pl.BlockSpec((1, 2, D_FF), lambda b, t: (b, 0, 0))],
        out_shape=[jax.ShapeDtypeStruct((n_seq * t_len, D_MODEL), F32),
                   jax.ShapeDtypeStruct((n_seq, 2, D_FF), F32)],
        scratch_shapes=[pltpu.VMEM((SUBLANES, D_FF), F32)],
        compiler_params=_cparams("arbitrary", "arbitrary"),
        name="ffn_prompt",
    )(x2d, n2, wup_b, wgate_b, cw, cb, wdown_b)


def _ffn_step(x2d, buf0, buf1, n2, wup_b, wgate_b, cw, cb, wdown_b):
    m = x2d.shape[0]
    args = (x2d, buf0, buf1, n2, wup_b, wgate_b, cw, cb, wdown_b)
    full = lambda a: pl.BlockSpec(a.shape, lambda i: (0,) * a.ndim)
    return pl.pallas_call(
        functools.partial(_ffn_kernel, tm=m, sequential=False),
        grid=(1,),
        in_specs=[full(a) for a in args],
        out_specs=[pl.BlockSpec((m, D_MODEL), lambda i: (0, 0)),
                   pl.BlockSpec((m, D_FF), lambda i: (0, 0)),
                   pl.BlockSpec((m, D_FF), lambda i: (0, 0))],
        out_shape=[jax.ShapeDtypeStruct((m, D_MODEL), F32),
                   jax.ShapeDtypeStruct((m, D_FF), F32),
                   jax.ShapeDtypeStruct((m, D_FF), F32)],
        compiler_params=_cparams("arbitrary"),
        name="ffn_step",
    )(*args)


def _s5_params(lam_re, lam_im, log_dt, b_re, b_im, c_re, c_im):
    lr, li = lam_re.astype(F32), lam_im.astype(F32)
    dt = jnp.exp(log_dt.astype(F32))[:, None]
    mag = jnp.exp(lr * dt)
    ar, ai = mag * jnp.cos(li * dt), mag * jnp.sin(li * dt)
    den = lr * lr + li * li
    fr = ((ar - 1.0) * lr + ai * li) / den
    fi = (ai * lr - (ar - 1.0) * li) / den
    br_, bi_ = b_re.astype(F32), b_im.astype(F32)
    bbr = fr[..., None] * br_ - fi[..., None] * bi_
    bbi = fr[..., None] * bi_ + fi[..., None] * br_
    eye = jnp.eye(SSM_GROUPS, dtype=F32)

    def bdiag_in(w):
        return jnp.einsum('gsc,gh->gchs', w, eye).reshape(SSM_WIDTH, N_STATE)

    def bdiag_out(w):
        return jnp.einsum('gcs,gh->gshc', w, eye).reshape(N_STATE, SSM_WIDTH)

    b_re_d, b_im_d = bdiag_in(bbr), bdiag_in(bbi)
    c_re_d, c_im_d = bdiag_out(c_re.astype(F32)), -bdiag_out(c_im.astype(F32))
    bblk = jnp.concatenate([b_re_d, b_im_d], axis=1).astype(BF16)
    cblk = jnp.concatenate([c_re_d, c_im_d], axis=0).astype(BF16)
    a_re, a_im = ar.reshape(1, N_STATE), ai.reshape(1, N_STATE)

    def cmul(x, y):
        return x[0] * y[0] - x[1] * y[1], x[0] * y[1] + x[1] * y[0]

    pows = [(jnp.ones_like(a_re), jnp.zeros_like(a_im))]
    for _ in range(SUBLANES):
        pows.append(cmul(pows[-1], (a_re, a_im)))
    n_in = S5_LANE_CHUNK // SSM_STATE * SSM_GROUP
    chunks = range(N_STATE // S5_LANE_CHUNK)
    cols = lambda m, c: m[:, c * S5_LANE_CHUNK:(c + 1) * S5_LANE_CHUNK]
    cut = lambda m, c: cols(m[c * n_in:(c + 1) * n_in], c)
    cut_t = lambda m, c: m[c * S5_LANE_CHUNK:(c + 1) * S5_LANE_CHUNK, c * n_in:(c + 1) * n_in]

    def lagged(c):
        bre, bim = cut(b_re_d, c), cut(b_im_d, c)
        re = [bre * cols(pr, c) - bim * cols(pi, c) for pr, pi in pows[:S5_MXU_LAGS]]
        im = [bre * cols(pi, c) + bim * cols(pr, c) for pr, pi in pows[:S5_MXU_LAGS]]
        return jnp.concatenate([jnp.concatenate(re, axis=0), jnp.concatenate(im, axis=0)], axis=1)

    bb4 = jnp.stack([lagged(c) for c in chunks]).astype(BF16)
    cc4 = jnp.stack([jnp.concatenate([cut_t(c_re_d, c), cut_t(c_im_d, c)], axis=0) for c in chunks]).astype(BF16)
    half = [(jnp.zeros_like(a_re), jnp.zeros_like(a_im))] * S5_MXU_LAGS + pows[1:SUBLANES - S5_MXU_LAGS + 1]
    tabs = [jnp.concatenate([p[0] for p in half], axis=0), jnp.concatenate([p[1] for p in half], axis=0),
            jnp.concatenate([p[0] for p in pows[1:]], axis=0), jnp.concatenate([p[1] for p in pows[1:]], axis=0)]
    return a_re, a_im, bblk, cblk, bb4, cc4, jnp.stack(tabs)


def kernel(x_prompt, x_sample, cache_k, cache_v, state_ssm_re, state_ssm_im, state_conv, page_table,
           norm1_w, w_in, q_norm_w, k_norm_w, lam_re, lam_im, log_dt, b_re, b_im, c_re, c_im, d_skip,
           w_glu, b_glu, w_a, w_b, w_o, norm2_w, w_up, w_gate, conv_w, conv_b, w_down):
    depth = norm1_w.shape[0]
    batch, seq, _ = x_prompt.shape
    dec_batch, dec_seq, _ = x_sample.shape
    assert depth == 1 and dec_seq == 1 and seq % MOBA_BLOCK == 0 and dec_batch <= LANES
    past_len = page_table.shape[1] * PAGE_SIZE
    slopes = 2.0 ** (-8.0 * jnp.arange(1, N_HEADS + 1, dtype=F32) / N_HEADS)
    seg = jnp.kron(jnp.eye(N_HEADS, dtype=F32), jnp.ones((HEAD_DIM, HEAD_DIM), F32)).astype(BF16)

    l = 0
    row = lambda a: a.reshape(1, -1).astype(F32)
    n1, n2 = row(norm1_w[l]), row(norm2_w[l])
    qn_row = row(jnp.tile(q_norm_w[l], N_HEADS))
    kn_row = row(jnp.tile(k_norm_w[l], N_HEADS))
    win_b = w_in[l].astype(BF16)
    a_re, a_im, bblk, cblk, bb4, cc4, tabs = _s5_params(lam_re[l], lam_im[l], log_dt[l], b_re[l], b_im[l], c_re[l], c_im[l])
    dskip, bglu = row(d_skip[l]), row(b_glu[l])
    wglu_b = w_glu[l].astype(BF16)
    wa_b, wb_b, wo_b = w_a[l].astype(BF16), w_b[l].astype(BF16), w_o[l].astype(BF16)
    wup_b, wgate_b, wdown_b = w_up[l].astype(BF16), w_gate[l].astype(BF16), w_down[l].astype(BF16)
    cw, cb = conv_w[l].astype(F32), row(conv_b[l])

    mp = batch * seq
    xp = x_prompt.reshape(mp, D_MODEL)
    u, q, gs, ga, k_t, v_t, kb, vta, kmean = _inproj(xp, n1, win_b, qn_row, kn_row, seg, batch, seq, 512, True)
    y_ssm, hre_p, him_p = _s5_prompt(u, bb4, cc4, tabs, dskip, wglu_b, bglu, batch, seq, 256)
    y_att = _moba_prompt(q.reshape(batch, seq, ATT_WIDTH), kb.reshape(batch, seq, ATT_WIDTH), vta,
                         kmean.reshape(batch, seq // MOBA_BLOCK, ATT_WIDTH), slopes)
    x1 = _merge(xp, y_ssm, y_att.reshape(mp, ATT_WIDTH), gs, ga, wa_b, wb_b, wo_b, 512)
    yp, conv_p = _ffn_prompt(x1, n2, wup_b, wgate_b, cw, cb, wdown_b, batch, seq, 512)

    xs = x_sample.reshape(dec_batch, D_MODEL)
    u_s, q_s, gs_s, ga_s, k_s, v_s = _inproj(xs, n1, win_b, qn_row, kn_row, seg, 1, dec_batch, dec_batch, False)
    y_ssm_s, hre_s, him_s = _s5_step(u_s, state_ssm_re[l].reshape(dec_batch, N_STATE),
                                     state_ssm_im[l].reshape(dec_batch, N_STATE),
                                     a_re, a_im, bblk, cblk, dskip, wglu_b, bglu)
    heads = lambda a: a.reshape(dec_batch, N_HEADS, HEAD_DIM)
    pad = lambda a_t: jnp.pad(a_t, ((0, 0), (0, LANES - dec_batch)))
    ck = jnp.transpose(cache_k[l], (0, 2, 3, 1))
    cv = jnp.transpose(cache_v[l], (0, 2, 3, 1))
    s_all, idx = _sample_scan(page_table, ck, pad(q_s.T), pad(k_s.T))
    y_att_s = _sample_attend(page_table, idx[..., 0], slopes, s_all, heads(q_s), heads(k_s), heads(v_s), cv, past_len)
    x1_s = _merge(xs, y_ssm_s, y_att_s.reshape(dec_batch, ATT_WIDTH), gs_s, ga_s, wa_b, wb_b, wo_b, dec_batch)
    ys, cs0, cs1 = _ffn_step(x1_s, state_conv[l, :, 0, :], state_conv[l, :, 1, :], n2, wup_b, wgate_b, cw, cb, wdown_b)

    g, s = SSM_GROUPS, SSM_STATE
    kv_out = lambda a_t: jnp.transpose(a_t.reshape(batch, N_HEADS, HEAD_DIM, seq), (0, 3, 1, 2))[None]
    return (yp.reshape(batch, seq, D_MODEL),
            ys.reshape(dec_batch, 1, D_MODEL),
            kv_out(k_t),
            kv_out(v_t),
            hre_p.reshape(1, batch, g, s),
            him_p.reshape(1, batch, g, s),
            conv_p.reshape(1, batch, 2, D_FF),
            k_s.reshape(1, dec_batch, 1, N_HEADS, HEAD_DIM),
            v_s.reshape(1, dec_batch, 1, N_HEADS, HEAD_DIM),
            hre_s.reshape(1, dec_batch, g, s),
            him_s.reshape(1, dec_batch, g, s),
            jnp.stack([cs0, cs1], axis=1).reshape(1, dec_batch, 2, D_FF))
```

```python
import functools

import jax
import jax.numpy as jnp
from jax import lax
from jax.experimental import pallas as pl
from jax.experimental.pallas import tpu as pltpu

F32 = jnp.float32
BF16 = jnp.bfloat16

D_MODEL = 1024
SSM_WIDTH = 512
SSM_GROUP = 16
SSM_GROUPS = 32
SSM_STATE = 64
N_STATE = SSM_GROUPS * SSM_STATE
N_HEADS = 8
HEAD_DIM = 64
ATT_WIDTH = 512
MOBA_BLOCK = 256
MOBA_TOPK = 3
D_FF = 2816
NORM_EPS = 1e-6
PAGE_SIZE = 128
LOG2E = 1.4426950408889634
NEG = -1e30

VMEM_LIMIT_BYTES = 56 * 1024 * 1024
SUBLANES = 8
LANES = 128


def _cparams(*sem):
    return pltpu.CompilerParams(dimension_semantics=sem, vmem_limit_bytes=VMEM_LIMIT_BYTES)


def _dot(a, b):
    return jnp.dot(a, b, preferred_element_type=F32)


def _dot_nt(a, b):
    return lax.dot_general(a, b, (((1,), (1,)), ((), ())), preferred_element_type=F32)


def _split_bf16(a):
    hi = a.astype(BF16)
    lo = (a - hi.astype(F32)).astype(BF16)
    return hi, lo


V_ROWS = 80


def _inproj_kernel(x_ref, n1_ref, win_ref, qn_ref, kn_ref, seg_ref, u_ref, q_ref, gs_ref, ga_ref, *rest,
                   tm, prompt):
    x = x_ref[...]
    ms = jnp.mean(x * x, axis=-1, keepdims=True)
    h = (x * lax.rsqrt(ms + NORM_EPS) * n1_ref[...]).astype(BF16)

    def proj(lo, hi):
        return _dot(h, win_ref[:, lo:hi])

    def head_rmsnorm(z, w_row):
        sq_hi, sq_lo = _split_bf16(z * z)
        ss = _dot(sq_hi, seg_ref[...]) + _dot(sq_lo, seg_ref[...])
        return z * lax.rsqrt(ss * (1.0 / HEAD_DIM) + NORM_EPS) * w_row

    o1 = SSM_WIDTH
    o2 = o1 + ATT_WIDTH
    o3 = o2 + ATT_WIDTH
    o4 = o3 + ATT_WIDTH
    o5 = o4 + D_MODEL
    u_ref[...] = proj(0, o1)
    q_ref[...] = head_rmsnorm(proj(o1, o2), qn_ref[...])
    k = head_rmsnorm(proj(o2, o3), kn_ref[...])
    v = proj(o3, o4)
    gs_ref[...] = proj(o4, o5)
    ga_ref[...] = proj(o5, o5 + D_MODEL)
    if not prompt:
        k_ref, v_ref = rest
        k_ref[...] = k
        v_ref[...] = v
        return
    kt_ref, vt_ref, kb_ref, vta_ref, kmean_ref = rest
    nb = tm // MOBA_BLOCK
    kt_ref[0] = k.T
    vt = v.T
    vt_ref[0] = vt
    kb_ref[...] = k.astype(BF16)
    ones = jnp.ones((V_ROWS - HEAD_DIM, MOBA_BLOCK), BF16)
    for hd in range(N_HEADS):
        for b in range(nb):
            vta_ref[0, b, hd * V_ROWS:hd * V_ROWS + HEAD_DIM, :] = (
                vt[hd * HEAD_DIM:(hd + 1) * HEAD_DIM, b * MOBA_BLOCK:(b + 1) * MOBA_BLOCK].astype(BF16))
            vta_ref[0, b, hd * V_ROWS + HEAD_DIM:(hd + 1) * V_ROWS, :] = ones
    ksum = k.reshape(nb, MOBA_BLOCK, ATT_WIDTH).sum(axis=1)
    kmean_ref[...] = (ksum * (1.0 / MOBA_BLOCK)).reshape(nb, 1, ATT_WIDTH)


def _inproj(x2d, n1, win_b, qn_row, kn_row, seg, n_seq, t_len, tm, prompt):
    nt = t_len // tm
    m = n_seq * t_len
    row = lambda w: pl.BlockSpec((tm, w), lambda b, t: (b * nt + t, 0))
    full = lambda a: pl.BlockSpec(a.shape, lambda b, t: (0,) * a.ndim)
    f32 = lambda *s: jax.ShapeDtypeStruct(s, F32)
    out_shape = [f32(m, SSM_WIDTH), f32(m, ATT_WIDTH), f32(m, D_MODEL), f32(m, D_MODEL)]
    out_specs = [row(SSM_WIDTH), row(ATT_WIDTH), row(D_MODEL), row(D_MODEL)]
    if prompt:
        nb = tm // MOBA_BLOCK
        tspec = pl.BlockSpec((1, ATT_WIDTH, tm), lambda b, t: (b, 0, t))
        out_shape += [f32(n_seq, ATT_WIDTH, t_len), f32(n_seq, ATT_WIDTH, t_len),
                      jax.ShapeDtypeStruct((m, ATT_WIDTH), BF16),
                      jax.ShapeDtypeStruct((n_seq, t_len // MOBA_BLOCK, N_HEADS * V_ROWS, MOBA_BLOCK), BF16),
                      f32(m // MOBA_BLOCK, 1, ATT_WIDTH)]
        out_specs += [tspec, tspec, row(ATT_WIDTH),
                      pl.BlockSpec((1, nb, N_HEADS * V_ROWS, MOBA_BLOCK), lambda b, t: (b, t, 0, 0)),
                      pl.BlockSpec((nb, 1, ATT_WIDTH), lambda b, t: (b * nt + t, 0, 0))]
    else:
        out_shape += [f32(m, ATT_WIDTH), f32(m, ATT_WIDTH)]
        out_specs += [row(ATT_WIDTH), row(ATT_WIDTH)]
    return pl.pallas_call(
        functools.partial(_inproj_kernel, tm=tm, prompt=prompt),
        grid=(n_seq, nt),
        in_specs=[row(D_MODEL), full(n1), full(win_b), full(qn_row), full(kn_row), full(seg)],
        out_specs=out_specs,
        out_shape=out_shape,
        compiler_params=_cparams("arbitrary", "arbitrary"),
        name="inproj",
    )(x2d, n1, win_b, qn_row, kn_row, seg)


def _s5_output(y_lin, u, dskip_ref, wglu_ref, bglu_ref):
    y = jax.nn.gelu(y_lin + dskip_ref[...] * u)
    gate = jax.nn.sigmoid(_dot(y.astype(BF16), wglu_ref[...]) + bglu_ref[...])
    return y * gate


S5_LANE_CHUNK = 512
S5_MXU_LAGS = 4


def _s5_prompt_kernel(u_ref, bb_ref, cc_ref, tabs_ref, dskip_ref, wglu_ref, bglu_ref,
                      y_ref, hre_ref, him_ref, xre, xim, cre, cim, *, tt):
    t = pl.program_id(1)

    @pl.when(t == 0)
    def _():
        cre[...] = jnp.zeros_like(cre)
        cim[...] = jnp.zeros_like(cim)

    u = u_ref[...]
    n_in = S5_LANE_CHUNK // SSM_STATE * SSM_GROUP
    lag_row = lax.broadcasted_iota(jnp.int32, (tt, 1), 0) % S5_MXU_LAGS
    y_parts = []
    for c in range(N_STATE // S5_LANE_CHUNK):
        ls = slice(c * S5_LANE_CHUNK, (c + 1) * S5_LANE_CHUNK)
        uc = u[:, c * n_in:(c + 1) * n_in]
        lagged = [uc.astype(BF16)]
        for lag in range(1, S5_MXU_LAGS):
            lagged.append(jnp.where(lag_row >= lag, pltpu.roll(uc, lag, 0), 0.0).astype(BF16))
        ustack = jnp.concatenate(lagged, axis=1)
        xre[:, ls] = _dot(ustack, bb_ref[c, :, :S5_LANE_CHUNK])
        xim[:, ls] = _dot(ustack, bb_ref[c, :, S5_LANE_CHUNK:])
        cr, ci = cre[:, ls], cim[:, ls]
        for b in range(tt // SUBLANES):
            rs = slice(b * SUBLANES, (b + 1) * SUBLANES)
            xr, xi = xre[rs, ls], xim[rs, ls]
            shape = (SUBLANES, S5_LANE_CHUNK)
            last = S5_MXU_LAGS - 1
            rr = jnp.broadcast_to(xr[last:last + 1, :], shape)
            ri = jnp.broadcast_to(xi[last:last + 1, :], shape)
            pr, pi = tabs_ref[0, :, ls], tabs_ref[1, :, ls]
            xr, xi = xr + (pr * rr - pi * ri), xi + (pr * ri + pi * rr)
            pr, pi = tabs_ref[2, :, ls], tabs_ref[3, :, ls]
            hr = xr + (pr * cr - pi * ci)
            hi = xi + (pr * ci + pi * cr)
            xre[rs, ls] = hr
            xim[rs, ls] = hi
            shape = (SUBLANES, S5_LANE_CHUNK)
            cr = jnp.broadcast_to(hr[SUBLANES - 1:SUBLANES, :], shape)
            ci = jnp.broadcast_to(hi[SUBLANES - 1:SUBLANES, :], shape)
        cre[:, ls] = cr
        cim[:, ls] = ci
        y_parts.append(_dot(xre[:, ls].astype(BF16), cc_ref[c, :S5_LANE_CHUNK, :])
                       + _dot(xim[:, ls].astype(BF16), cc_ref[c, S5_LANE_CHUNK:, :]))

    hre_ref[0] = cre[0:1, :]
    him_ref[0] = cim[0:1, :]
    y_ref[...] = _s5_output(jnp.concatenate(y_parts, axis=1), u, dskip_ref, wglu_ref, bglu_ref)


def _s5_prompt(u2d, bblk, cblk, tabs, dskip, wglu_b, bglu, n_seq, t_len, tt):
    nt = t_len // tt
    full = lambda a: pl.BlockSpec(a.shape, lambda b, t: (0,) * a.ndim)
    return pl.pallas_call(
        functools.partial(_s5_prompt_kernel, tt=tt),
        grid=(n_seq, nt),
        in_specs=[pl.BlockSpec((tt, SSM_WIDTH), lambda b, t: (b * nt + t, 0)),
                  full(bblk), full(cblk), full(tabs), full(dskip), full(wglu_b), full(bglu)],
        out_specs=[pl.BlockSpec((tt, SSM_WIDTH), lambda b, t: (b * nt + t, 0)),
                   pl.BlockSpec((1, 1, N_STATE), lambda b, t: (b, 0, 0)),
                   pl.BlockSpec((1, 1, N_STATE), lambda b, t: (b, 0, 0))],
        out_shape=[jax.ShapeDtypeStruct((n_seq * t_len, SSM_WIDTH), F32),
                   jax.ShapeDtypeStruct((n_seq, 1, N_STATE), F32),
                   jax.ShapeDtypeStruct((n_seq, 1, N_STATE), F32)],
        scratch_shapes=[pltpu.VMEM((tt, N_STATE), F32), pltpu.VMEM((tt, N_STATE), F32),
                        pltpu.VMEM((SUBLANES, N_STATE), F32), pltpu.VMEM((SUBLANES, N_STATE), F32)],
        compiler_params=_cparams("arbitrary", "arbitrary"),
        name="s5_prompt",
    )(u2d, bblk, cblk, tabs, dskip, wglu_b, bglu)


def _s5_step_kernel(u_ref, h0re_ref, h0im_ref, are_ref, aim_ref, bb_ref, cc_ref,
                    dskip_ref, wglu_ref, bglu_ref, y_ref, hre_ref, him_ref):
    u = u_ref[...]
    ub = u.astype(BF16)
    n_in = S5_LANE_CHUNK // SSM_STATE * SSM_GROUP
    y_parts = []
    for c in range(N_STATE // S5_LANE_CHUNK):
        ls = slice(c * S5_LANE_CHUNK, (c + 1) * S5_LANE_CHUNK)
        us = ub[:, c * n_in:(c + 1) * n_in]
        ar, ai = are_ref[:, ls], aim_ref[:, ls]
        h0r, h0i = h0re_ref[:, ls], h0im_ref[:, ls]
        hr = _dot(us, bb_ref[c, :n_in, :S5_LANE_CHUNK]) + (ar * h0r - ai * h0i)
        hi = _dot(us, bb_ref[c, :n_in, S5_LANE_CHUNK:]) + (ar * h0i + ai * h0r)
        hre_ref[:, ls] = hr
        him_ref[:, ls] = hi
        y_parts.append(_dot(hr.astype(BF16), cc_ref[c, :S5_LANE_CHUNK, :])
                       + _dot(hi.astype(BF16), cc_ref[c, S5_LANE_CHUNK:, :]))
    y_ref[...] = _s5_output(jnp.concatenate(y_parts, axis=1), u, dskip_ref, wglu_ref, bglu_ref)


def _s5_step(u2d, h0re, h0im, are, aim, bblk, cblk, dskip, wglu_b, bglu):
    m = u2d.shape[0]
    args = (u2d, h0re, h0im, are, aim, bblk, cblk, dskip, wglu_b, bglu)
    full = lambda a: pl.BlockSpec(a.shape, lambda i: (0,) * a.ndim)
    return pl.pallas_call(
        _s5_step_kernel,
        grid=(1,),
        in_specs=[full(a) for a in args],
        out_specs=[pl.BlockSpec((m, SSM_WIDTH), lambda i: (0, 0)),
                   pl.BlockSpec((m, N_STATE), lambda i: (0, 0)),
                   pl.BlockSpec((m, N_STATE), lambda i: (0, 0))],
        out_shape=[jax.ShapeDtypeStruct((m, SSM_WIDTH), F32),
                   jax.ShapeDtypeStruct((m, N_STATE), F32),
                   jax.ShapeDtypeStruct((m, N_STATE), F32)],
        compiler_params=_cparams("arbitrary"),
        name="s5_step",
    )(*args)


def _topk_bias_t(gate_t, nblk):
    bidx = lax.broadcasted_iota(jnp.int32, gate_t.shape, 0).astype(F32)
    bias = jnp.full(gate_t.shape, NEG, F32)
    g = gate_t
    for _ in range(MOBA_TOPK):
        mx = jnp.max(g, axis=0, keepdims=True)
        am = jnp.min(jnp.where(g == mx, bidx, float(nblk)), axis=0, keepdims=True)
        hit = bidx == am
        bias = jnp.where(hit & (mx > -jnp.inf), 0.0, bias)
        g = jnp.where(hit, -jnp.inf, g)
    return bias


ALIBI_PARTS = 3
SCORE_LOOKAHEAD = 6
PAST_BLOCKS_PER_TRIP = 4


def _moba_prompt_kernel(q_ref, kb_ref, vta_ref, kmean_ref, slope_ref, o_ref,
                        qt_sc, sel_sc, m_sc, acc_sc, ak_sc, *, nblk):
    i = pl.program_id(1)
    bq = MOBA_BLOCK
    key_idx = lax.broadcasted_iota(jnp.int32, (bq, bq), 0)
    qry_idx = lax.broadcasted_iota(jnp.int32, (bq, bq), 1)
    lane = lax.broadcasted_iota(jnp.int32, (1, LANES), 1)
    sub = lax.broadcasted_iota(jnp.int32, (LANES, 1), 0)
    gblk = lax.broadcasted_iota(jnp.int32, (nblk, bq), 0)


    @pl.when((pl.program_id(0) == 0) & (i == 0))
    def _():
        key_pos = lax.broadcasted_iota(jnp.int32, (bq, LANES), 0).astype(F32)
        lane_b = lax.broadcasted_iota(jnp.int32, (bq, LANES), 1)
        for p in range(N_HEADS // 2):
            slope = jnp.where(lane_b // ALIBI_PARTS == 0, slope_ref[2 * p], slope_ref[2 * p + 1]) * LOG2E
            rest = slope * key_pos
            tile = jnp.zeros((bq, LANES), F32)
            for k in range(ALIBI_PARTS):
                part = rest.astype(BF16).astype(F32)
                tile = jnp.where(lane_b % ALIBI_PARTS == k, part, tile)
                rest = rest - part
            ak_sc[p] = jnp.where(lane_b < 2 * ALIBI_PARTS, tile, 0.0).astype(BF16)
        for h in range(N_HEADS):
            qt_sc[h, LANES:, :] = jnp.broadcast_to(
                jnp.where((sub // ALIBI_PARTS) == h % 2, 1.0, 0.0).astype(BF16), (LANES, bq))

    for p in range(N_HEADS // 2):
        ls = slice(p * LANES, (p + 1) * LANES)
        qp = q_ref[0, :, ls]
        q_hi, q_lo = _split_bf16(qp)
        qt = qp.T * (HEAD_DIM ** -0.5 * LOG2E)
        km = kmean_ref[0, :, ls]
        for hh in range(2):
            h = 2 * p + hh
            qt_sc[h, :LANES, :] = jnp.where((sub // HEAD_DIM) == hh, qt, 0.0).astype(BF16)
            km_hi, km_lo = _split_bf16(jnp.where((lane // HEAD_DIM) == hh, km, 0.0))
            gate_t = _dot_nt(km_hi, q_hi) + _dot_nt(km_hi, q_lo) + _dot_nt(km_lo, q_hi)
            sel_sc[h] = _topk_bias_t(jnp.where(gblk < i, gate_t, -jnp.inf), nblk)

    def scores(j, h):
        c0 = pl.multiple_of(j * bq, bq)
        kj = kb_ref[0, pl.ds(c0, bq), (h // 2) * LANES:(h // 2 + 1) * LANES]
        return _dot(jnp.concatenate([kj, ak_sc[h // 2]], axis=1), qt_sc[h])

    def values(j, h):
        return vta_ref[0, j, h * V_ROWS:(h + 1) * V_ROWS, :]

    def pipelined(blocks, update_fn):
        tasks = [(j, h) for j in blocks for h in range(N_HEADS)]
        pending = [scores(*t) for t in tasks[:SCORE_LOOKAHEAD]]
        for k, (j, h) in enumerate(tasks):
            if k + SCORE_LOOKAHEAD < len(tasks):
                pending.append(scores(*tasks[k + SCORE_LOOKAHEAD]))
            update_fn(j, h, pending.pop(0))

    def own_update(j, h, s_t):
        s_t = jnp.where(key_idx <= qry_idx, s_t, NEG)
        m = jnp.max(s_t, axis=0, keepdims=True)
        m_sc[h] = jnp.broadcast_to(m, (SUBLANES, bq))
        acc_sc[h] = _dot(values(j, h), jnp.exp2(s_t - m).astype(BF16))

    pipelined([i], own_update)

    def past_update(j, h, s_t):
        bias = sel_sc[h, pl.ds(j, 1), :] + (slope_ref[h] * LOG2E) * ((j - i) * bq).astype(F32)
        m_old = m_sc[h][0:1, :]
        m_new = jnp.maximum(m_old, jnp.max(s_t, axis=0, keepdims=True) + bias)
        m_sc[h] = jnp.broadcast_to(m_new, (SUBLANES, bq))
        pv = _dot(values(j, h), jnp.exp2(s_t - (m_new - bias)).astype(BF16))
        acc_sc[h] = jnp.exp2(m_old - m_new) * acc_sc[h] + pv

    def body(t, carry):
        pipelined([PAST_BLOCKS_PER_TRIP * t + d for d in range(PAST_BLOCKS_PER_TRIP)], past_update)
        return carry

    n_trips = i // PAST_BLOCKS_PER_TRIP
    lax.fori_loop(0, n_trips, body, 0)
    j_left = n_trips * PAST_BLOCKS_PER_TRIP
    width = PAST_BLOCKS_PER_TRIP // 2
    while width:
        @pl.when((i & width) != 0)
        def _(j_left=j_left, width=width):
            pipelined([j_left + d for d in range(width)], past_update)

        j_left = j_left + (i & width)
        width //= 2

    for p in range(N_HEADS // 2):
        halves = []
        for h in (2 * p, 2 * p + 1):
            acc = acc_sc[h]
            halves.append(acc[:HEAD_DIM, :] / acc[HEAD_DIM:HEAD_DIM + 1, :])
        o_ref[0, :, p * LANES:(p + 1) * LANES] = jnp.concatenate(halves, axis=0).T


def _moba_prompt(q3, kb3, vta, kmean3, slopes):
    n, t_len, _ = q3.shape
    nblk = t_len // MOBA_BLOCK
    return pl.pallas_call(
        functools.partial(_moba_prompt_kernel, nblk=nblk),
        grid=(n, nblk),
        in_specs=[pl.BlockSpec((1, MOBA_BLOCK, ATT_WIDTH), lambda b, i: (b, i, 0)),
                  pl.BlockSpec((1, t_len, ATT_WIDTH), lambda b, i: (b, 0, 0)),
                  pl.BlockSpec((1, nblk, N_HEADS * V_ROWS, MOBA_BLOCK), lambda b, i: (b, 0, 0, 0)),
                  pl.BlockSpec((1, nblk, ATT_WIDTH), lambda b, i: (b, 0, 0)),
                  pl.BlockSpec(memory_space=pltpu.SMEM)],
        out_specs=pl.BlockSpec((1, MOBA_BLOCK, ATT_WIDTH), lambda b, i: (b, i, 0)),
        out_shape=jax.ShapeDtypeStruct((n, t_len, ATT_WIDTH), F32),
        scratch_shapes=[pltpu.VMEM((N_HEADS, 2 * LANES, MOBA_BLOCK), BF16),
                        pltpu.VMEM((N_HEADS, nblk, MOBA_BLOCK), F32),
                        pltpu.VMEM((N_HEADS, SUBLANES, MOBA_BLOCK), F32),
                        pltpu.VMEM((N_HEADS, V_ROWS, MOBA_BLOCK), F32),
                        pltpu.VMEM((N_HEADS // 2, MOBA_BLOCK, LANES), BF16)],
        compiler_params=_cparams("arbitrary", "arbitrary"),
        name="moba_prompt",
    )(q3, kb3, vta, kmean3, slopes)


PAGES_PER_STEP = 32
PAGES_PER_BLOCK = MOBA_BLOCK // PAGE_SIZE
BLOCKS_PER_STEP = PAGES_PER_STEP // PAGES_PER_BLOCK


def _sample_scan_kernel(pt_ref, *refs, n_past):
    pages = refs[:PAGES_PER_STEP]
    qt_ref, knt_ref, s_ref, idx_ref, qrep_sc, g_sc = refs[PAGES_PER_STEP:]
    n = pl.program_id(0)
    c = pl.program_id(1)
    lane = lax.broadcasted_iota(jnp.int32, (1, LANES), 1)

    @pl.when(c == 0)
    def _():
        qcol = jnp.sum(jnp.where(lane == n, qt_ref[...], 0.0), axis=-1, keepdims=True)
        qrep_sc[...] = jnp.broadcast_to(qcol, qrep_sc.shape)
        g_sc[...] = jnp.zeros_like(g_sc)

    qrep = qrep_sc[...].reshape(N_HEADS, HEAD_DIM, LANES)
    g = g_sc[...]
    for r in range(0, PAGES_PER_STEP, PAGES_PER_BLOCK):
        blk = jnp.zeros((N_HEADS, LANES), F32)
        for rr in range(r, r + PAGES_PER_BLOCK):
            sc = jnp.sum(pages[rr][0] * qrep, axis=1)
            s_ref[0, rr] = sc
            blk = blk + sc
        g = jnp.where(lane == c * BLOCKS_PER_STEP + r // PAGES_PER_BLOCK,
                      jnp.sum(blk, axis=-1, keepdims=True), g)
    g_sc[...] = g

    @pl.when(c == pl.num_programs(1) - 1)
    def _():
        kcol = jnp.sum(jnp.where(lane == n, knt_ref[...], 0.0), axis=-1, keepdims=True)
        g_own = jnp.sum((qrep_sc[:, 0:1] * kcol).reshape(N_HEADS, HEAD_DIM, 1), axis=1)
        gate = jnp.where(lane == n_past, g_own, g) * (1.0 / MOBA_BLOCK)
        gsel = jnp.where(lane < n_past, gate, -jnp.inf)
        lanef = lane.astype(F32)
        for k in range(MOBA_TOPK):
            mx = jnp.max(gsel, axis=-1, keepdims=True)
            am = jnp.min(jnp.where(gsel == mx, lanef, float(LANES)), axis=-1, keepdims=True)
            idx_ref[0, k] = jnp.broadcast_to(am.astype(jnp.int32), (N_HEADS, LANES))
            gsel = jnp.where(lanef == am, -jnp.inf, gsel)


def _sample_scan(page_table, ck, q_t, knew_t):
    n, n_pages = page_table.shape
    steps = n_pages // PAGES_PER_STEP
    n_past = n_pages // PAGES_PER_BLOCK

    def page_spec(r):
        return pl.BlockSpec((1, N_HEADS, HEAD_DIM, PAGE_SIZE),
                            lambda b, c, pt: (pt[b * n_pages + c * PAGES_PER_STEP + r], 0, 0, 0))

    full = lambda a: pl.BlockSpec(a.shape, lambda b, c, pt: (0,) * a.ndim)
    return pl.pallas_call(
        functools.partial(_sample_scan_kernel, n_past=n_past),
        grid_spec=pltpu.PrefetchScalarGridSpec(
            num_scalar_prefetch=1,
            grid=(n, steps),
            in_specs=[page_spec(r) for r in range(PAGES_PER_STEP)] + [full(q_t), full(knew_t)],
            out_specs=[pl.BlockSpec((1, PAGES_PER_STEP, N_HEADS, PAGE_SIZE), lambda b, c, pt: (b, c, 0, 0)),
                       pl.BlockSpec((1, MOBA_TOPK, N_HEADS, LANES), lambda b, c, pt: (b, 0, 0, 0))],
            scratch_shapes=[pltpu.VMEM((ATT_WIDTH, LANES), F32), pltpu.VMEM((N_HEADS, LANES), F32)],
        ),
        out_shape=[jax.ShapeDtypeStruct((n, n_pages, N_HEADS, PAGE_SIZE), F32),
                   jax.ShapeDtypeStruct((n, MOBA_TOPK, N_HEADS, LANES), jnp.int32)],
        compiler_params=_cparams("arbitrary", "arbitrary"),
        name="sample_scan",
    )(page_table.reshape(-1), *([ck] * PAGES_PER_STEP), q_t, knew_t)


SEL_PAGES = MOBA_TOPK * PAGES_PER_BLOCK


ATTEND_HEADS_PER_STEP = 4


def _sample_attend_kernel(pt_ref, ix_ref, slope_ref, s_ref, q_ref, knew_ref, vnew_ref, *refs, past_len):
    n_vt = ATTEND_HEADS_PER_STEP * SEL_PAGES
    vt_refs, o_ref = refs[:n_vt], refs[n_vt]
    n = pl.program_id(0)
    scale = HEAD_DIM ** -0.5
    off = lax.broadcasted_iota(jnp.int32, (1, PAGE_SIZE), 1).astype(F32)
    for hl in range(ATTEND_HEADS_PER_STEP):
        h = pl.program_id(1) * ATTEND_HEADS_PER_STEP + hl
        slope = slope_ref[h]
        qh = q_ref[0, pl.ds(h, 1), :]
        s0 = jnp.sum(qh * knew_ref[0, pl.ds(h, 1), :], axis=-1, keepdims=True) * scale
        rows = []
        for s in range(MOBA_TOPK):
            blk = ix_ref[(n * MOBA_TOPK + s) * N_HEADS + h]
            for half in range(PAGES_PER_BLOCK):
                raw = s_ref[0, blk * PAGES_PER_BLOCK + half, pl.ds(h, 1), :]
                dist = (past_len - blk * MOBA_BLOCK - half * PAGE_SIZE).astype(F32) - off
                rows.append(raw * scale - slope * dist)
        sc = jnp.concatenate(rows, axis=1)
        m = jnp.maximum(s0, jnp.max(sc, axis=-1, keepdims=True))
        p = jnp.exp(sc - m)
        p0 = jnp.exp(s0 - m)
        l = p0 + jnp.sum(p, axis=-1, keepdims=True)
        vt = jnp.concatenate([vt_refs[hl * SEL_PAGES + w][0, 0] for w in range(SEL_PAGES)], axis=1)
        pv = _dot_nt(jnp.broadcast_to(p, (SUBLANES, p.shape[1])).astype(BF16), vt.astype(BF16))[0:1]
        o_ref[0, pl.ds(h, 1), :] = (p0 * vnew_ref[0, pl.ds(h, 1), :] + pv) / l


def _sample_attend(page_table, idx, slopes, s_all, q3, knew3, vnew3, cv, past_len):
    n = q3.shape[0]
    n_pages = page_table.shape[1]

    def vt_spec(hl, w):
        def index(b, g, pt, ix, sl):
            h = g * ATTEND_HEADS_PER_STEP + hl
            blk = ix[(b * MOBA_TOPK + w // PAGES_PER_BLOCK) * N_HEADS + h]
            return (pt[b * n_pages + blk * PAGES_PER_BLOCK + w % PAGES_PER_BLOCK], h, 0, 0)
        return pl.BlockSpec((1, 1, HEAD_DIM, PAGE_SIZE), index)

    tok = pl.BlockSpec((1, N_HEADS, HEAD_DIM), lambda b, g, pt, ix, sl: (b, 0, 0))
    n_vt = ATTEND_HEADS_PER_STEP * SEL_PAGES
    return pl.pallas_call(
        functools.partial(_sample_attend_kernel, past_len=past_len),
        grid_spec=pltpu.PrefetchScalarGridSpec(
            num_scalar_prefetch=3,
            grid=(n, N_HEADS // ATTEND_HEADS_PER_STEP),
            in_specs=[pl.BlockSpec((1,) + s_all.shape[1:], lambda b, g, pt, ix, sl: (b, 0, 0, 0)), tok, tok, tok]
                     + [vt_spec(hl, w) for hl in range(ATTEND_HEADS_PER_STEP) for w in range(SEL_PAGES)],
            out_specs=tok,
        ),
        out_shape=jax.ShapeDtypeStruct((n, N_HEADS, HEAD_DIM), F32),
        compiler_params=_cparams("arbitrary", "arbitrary"),
        name="sample_attend",
    )(page_table.reshape(-1), idx.reshape(-1), slopes, s_all, q3, knew3, vnew3, *([cv] * n_vt))


def _merge_kernel(x_ref, ys_ref, ya_ref, gs_ref, ga_ref, wa_ref, wb_ref, wo_ref, o_ref):
    a = _dot(ys_ref[...].astype(BF16), wa_ref[...])
    b = _dot(ya_ref[...].astype(BF16), wb_ref[...])
    mix = jax.nn.sigmoid(gs_ref[...]) * a + jax.nn.sigmoid(ga_ref[...]) * b
    o_ref[...] = x_ref[...] + _dot(mix.astype(BF16), wo_ref[...])


def _merge(x2d, ys, ya, gs, ga, wa_b, wb_b, wo_b, tm):
    m = x2d.shape[0]
    row = lambda w: pl.BlockSpec((tm, w), lambda i: (i, 0))
    full = lambda a: pl.BlockSpec(a.shape, lambda i: (0,) * a.ndim)
    return pl.pallas_call(
        _merge_kernel,
        grid=(m // tm,),
        in_specs=[row(D_MODEL), row(SSM_WIDTH), row(ATT_WIDTH), row(D_MODEL), row(D_MODEL),
                  full(wa_b), full(wb_b), full(wo_b)],
        out_specs=row(D_MODEL),
        out_shape=jax.ShapeDtypeStruct((m, D_MODEL), F32),
        compiler_params=_cparams("arbitrary"),
        name="merge",
    )(x2d, ys, ya, gs, ga, wa_b, wb_b, wo_b)


FF_CHUNK = D_FF // 2


def _ffn_kernel(x_ref, *refs, tm, sequential):
    if sequential:
        (n2_ref, wup_ref, wgate_ref, cw_ref, cb_ref, wdown_ref, o_ref, cs_ref, tail) = refs
    else:
        (buf0_ref, buf1_ref, n2_ref, wup_ref, wgate_ref, cw_ref, cb_ref, wdown_ref, o_ref, cs0_ref, cs1_ref) = refs
    x = x_ref[...]
    ms = jnp.mean(x * x, axis=-1, keepdims=True)
    h = (x * lax.rsqrt(ms + NORM_EPS) * n2_ref[...]).astype(BF16)

    if sequential:
        @pl.when(pl.program_id(1) == 0)
        def _():
            tail[...] = jnp.zeros_like(tail)
        rows = lax.broadcasted_iota(jnp.int32, (tm, 1), 0)

    y = x
    for c in range(D_FF // FF_CHUNK):
        cs = slice(c * FF_CHUNK, (c + 1) * FF_CHUNK)
        up = _dot(h, wup_ref[:, cs])
        g = _dot(h, wgate_ref[:, cs])
        if sequential:
            p2 = tail[SUBLANES - 2:SUBLANES - 1, cs]
            p1 = tail[SUBLANES - 1:SUBLANES, cs]
            g1 = jnp.where(rows == 0, p1, pltpu.roll(g, 1, 0))
            g2 = jnp.where(rows == 0, p2, jnp.where(rows == 1, p1, pltpu.roll(g, 2, 0)))
            tail[:, cs] = g[tm - SUBLANES:, :]
            cs_ref[0, :, cs] = g[tm - 2:, :]
        else:
            g2 = buf0_ref[:, cs]
            g1 = buf1_ref[:, cs]
            cs0_ref[:, cs] = g1
            cs1_ref[:, cs] = g
        conv = cb_ref[:, cs] + cw_ref[0:1, cs] * g2 + cw_ref[1:2, cs] * g1 + cw_ref[2:3, cs] * g
        act = (jax.nn.gelu(conv) * up).astype(BF16)
        y = y + _dot(act, wdown_ref[cs, :])
    o_ref[...] = y


def _ffn_prompt(x2d, n2, wup_b, wgate_b, cw, cb, wdown_b, n_seq, t_len, tm):
    nt = t_len // tm
    full = lambda a: pl.BlockSpec(a.shape, lambda b, t: (0,) * a.ndim)
    row = pl.BlockSpec((tm, D_MODEL), lambda b, t: (b * nt + t, 0))
    return pl.pallas_call(
        functools.partial(_ffn_kernel, tm=tm, sequential=True),
        grid=(n_seq, nt),
        in_specs=[row, full(n2), full(wup_b), full(wgate_b), full(cw), full(cb), full(wdown_b)],
        out_specs=[row, pl.BlockSpec((1, 2, D_FF), lambda b, t: (b, 0, 0))],
        out_shape=[jax.ShapeDtypeStruct((n_seq * t_len, D_MODEL), F32),
                   jax.ShapeDtypeStruct((n_seq, 2, D_FF), F32)],
        scratch_shapes=[pltpu.VMEM((SUBLANES, D_FF), F32)],
        compiler_params=_cparams("arbitrary", "arbitrary"),
        name="ffn_prompt",
    )(x2d, n2, wup_b, wgate_b, cw, cb, wdown_b)


def _ffn_step(x2d, buf0, buf1, n2, wup_b, wgate_b, cw, cb, wdown_b):
    m = x2d.shape[0]
    args = (x2d, buf0, buf1, n2, wup_b, wgate_b, cw, cb, wdown_b)
    full = lambda a: pl.BlockSpec(a.shape, lambda i: (0,) * a.ndim)
    return pl.pallas_call(
        functools.partial(_ffn_kernel, tm=m, sequential=False),
        grid=(1,),
        in_specs=[full(a) for a in args],
        out_specs=[pl.BlockSpec((m, D_MODEL), lambda i: (0, 0)),
                   pl.BlockSpec((m, D_FF), lambda i: (0, 0)),
                   pl.BlockSpec((m, D_FF), lambda i: (0, 0))],
        out_shape=[jax.ShapeDtypeStruct((m, D_MODEL), F32),
                   jax.ShapeDtypeStruct((m, D_FF), F32),
                   jax.ShapeDtypeStruct((m, D_FF), F32)],
        compiler_params=_cparams("arbitrary"),
        name="ffn_step",
    )(*args)


def _s5_params(lam_re, lam_im, log_dt, b_re, b_im, c_re, c_im):
    lr, li = lam_re.astype(F32), lam_im.astype(F32)
    dt = jnp.exp(log_dt.astype(F32))[:, None]
    mag = jnp.exp(lr * dt)
    ar, ai = mag * jnp.cos(li * dt), mag * jnp.sin(li * dt)
    den = lr * lr + li * li
    fr = ((ar - 1.0) * lr + ai * li) / den
    fi = (ai * lr - (ar - 1.0) * li) / den
    br_, bi_ = b_re.astype(F32), b_im.astype(F32)
    bbr = fr[..., None] * br_ - fi[..., None] * bi_
    bbi = fr[..., None] * bi_ + fi[..., None] * br_
    n_ch = N_STATE // S5_LANE_CHUNK
    g_ch = SSM_GROUPS // n_ch
    n_in = g_ch * SSM_GROUP
    eye = jnp.eye(g_ch, dtype=F32)

    def bdiag_in(w):
        w4 = w.reshape(n_ch, g_ch, SSM_STATE, SSM_GROUP)
        return jnp.einsum('kgsc,gh->kgchs', w4, eye).reshape(n_ch, n_in, S5_LANE_CHUNK)

    def bdiag_out(w):
        w4 = w.reshape(n_ch, g_ch, SSM_GROUP, SSM_STATE)
        return jnp.einsum('kgcs,gh->kgshc', w4, eye).reshape(n_ch, S5_LANE_CHUNK, n_in)

    b_re_c, b_im_c = bdiag_in(bbr), bdiag_in(bbi)
    a_re, a_im = ar.reshape(1, N_STATE), ai.reshape(1, N_STATE)

    def cmul(x, y):
        return x[0] * y[0] - x[1] * y[1], x[0] * y[1] + x[1] * y[0]

    pows = [(jnp.ones_like(a_re), jnp.zeros_like(a_im))]
    for _ in range(SUBLANES):
        pows.append(cmul(pows[-1], (a_re, a_im)))
    per_chunk = lambda row: row.reshape(n_ch, 1, S5_LANE_CHUNK)
    lag_re = [b_re_c * per_chunk(pr) - b_im_c * per_chunk(pi) for pr, pi in pows[:S5_MXU_LAGS]]
    lag_im = [b_re_c * per_chunk(pi) + b_im_c * per_chunk(pr) for pr, pi in pows[:S5_MXU_LAGS]]
    bb4 = jnp.concatenate([jnp.concatenate(lag_re, axis=1), jnp.concatenate(lag_im, axis=1)], axis=2).astype(BF16)
    cc4 = jnp.concatenate([bdiag_out(c_re.astype(F32)), -bdiag_out(c_im.astype(F32))], axis=1).astype(BF16)
    half = [(jnp.zeros_like(a_re), jnp.zeros_like(a_im))] * S5_MXU_LAGS + pows[1:SUBLANES - S5_MXU_LAGS + 1]
    tabs = [jnp.concatenate([p[0] for p in half], axis=0), jnp.concatenate([p[1] for p in half], axis=0),
            jnp.concatenate([p[0] for p in pows[1:]], axis=0), jnp.concatenate([p[1] for p in pows[1:]], axis=0)]
    return a_re, a_im, bb4, cc4, jnp.stack(tabs)


def kernel(x_prompt, x_sample, cache_k, cache_v, state_ssm_re, state_ssm_im, state_conv, page_table,
           norm1_w, w_in, q_norm_w, k_norm_w, lam_re, lam_im, log_dt, b_re, b_im, c_re, c_im, d_skip,
           w_glu, b_glu, w_a, w_b, w_o, norm2_w, w_up, w_gate, conv_w, conv_b, w_down):
    depth = norm1_w.shape[0]
    batch, seq, _ = x_prompt.shape
    dec_batch, dec_seq, _ = x_sample.shape
    assert depth == 1 and dec_seq == 1 and seq % MOBA_BLOCK == 0 and dec_batch <= LANES
    past_len = page_table.shape[1] * PAGE_SIZE
    slopes = 2.0 ** (-8.0 * jnp.arange(1, N_HEADS + 1, dtype=F32) / N_HEADS)
    seg = jnp.kron(jnp.eye(N_HEADS, dtype=F32), jnp.ones((HEAD_DIM, HEAD_DIM), F32)).astype(BF16)

    l = 0
    row = lambda a: a.reshape(1, -1).astype(F32)
    n1, n2 = row(norm1_w[l]), row(norm2_w[l])
    qn_row = row(jnp.tile(q_norm_w[l], N_HEADS))
    kn_row = row(jnp.tile(k_norm_w[l], N_HEADS))
    win_b = w_in[l].astype(BF16)
    a_re, a_im, bb4, cc4, tabs = _s5_params(lam_re[l], lam_im[l], log_dt[l], b_re[l], b_im[l], c_re[l], c_im[l])
    dskip, bglu = row(d_skip[l]), row(b_glu[l])
    wglu_b = w_glu[l].astype(BF16)
    wa_b, wb_b, wo_b = w_a[l].astype(BF16), w_b[l].astype(BF16), w_o[l].astype(BF16)
    wup_b, wgate_b, wdown_b = w_up[l].astype(BF16), w_gate[l].astype(BF16), w_down[l].astype(BF16)
    cw, cb = conv_w[l].astype(F32), row(conv_b[l])

    mp = batch * seq
    xp = x_prompt.reshape(mp, D_MODEL)
    u, q, gs, ga, k_t, v_t, kb, vta, kmean = _inproj(xp, n1, win_b, qn_row, kn_row, seg, batch, seq, 512, True)
    y_ssm, hre_p, him_p = _s5_prompt(u, bb4, cc4, tabs, dskip, wglu_b, bglu, batch, seq, 256)
    y_att = _moba_prompt(q.reshape(batch, seq, ATT_WIDTH), kb.reshape(batch, seq, ATT_WIDTH), vta,
                         kmean.reshape(batch, seq // MOBA_BLOCK, ATT_WIDTH), slopes)
    x1 = _merge(xp, y_ssm, y_att.reshape(mp, ATT_WIDTH), gs, ga, wa_b, wb_b, wo_b, 512)
    yp, conv_p = _ffn_prompt(x1, n2, wup_b, wgate_b, cw, cb, wdown_b, batch, seq, 512)

    xs = x_sample.reshape(dec_batch, D_MODEL)
    u_s, q_s, gs_s, ga_s, k_s, v_s = _inproj(xs, n1, win_b, qn_row, kn_row, seg, 1, dec_batch, dec_batch, False)
    y_ssm_s, hre_s, him_s = _s5_step(u_s, state_ssm_re[l].reshape(dec_batch, N_STATE),
                                     state_ssm_im[l].reshape(dec_batch, N_STATE),
                                     a_re, a_im, bb4, cc4, dskip, wglu_b, bglu)
    heads = lambda a: a.reshape(dec_batch, N_HEADS, HEAD_DIM)
    pad = lambda a_t: jnp.pad(a_t, ((0, 0), (0, LANES - dec_batch)))
    ck = jnp.transpose(cache_k[l], (0, 2, 3, 1))
    cv = jnp.transpose(cache_v[l], (0, 2, 3, 1))
    s_all, idx = _sample_scan(page_table, ck, pad(q_s.T), pad(k_s.T))
    y_att_s = _sample_attend(page_table, idx[..., 0], slopes, s_all, heads(q_s), heads(k_s), heads(v_s), cv, past_len)
    x1_s = _merge(xs, y_ssm_s, y_att_s.reshape(dec_batch, ATT_WIDTH), gs_s, ga_s, wa_b, wb_b, wo_b, dec_batch)
    ys, cs0, cs1 = _ffn_step(x1_s, state_conv[l, :, 0, :], state_conv[l, :, 1, :], n2, wup_b, wgate_b, cw, cb, wdown_b)

    g, s = SSM_GROUPS, SSM_STATE
    kv_out = lambda a_t: jnp.transpose(a_t.reshape(batch, N_HEADS, HEAD_DIM, seq), (0, 3, 1, 2))[None]
    return (yp.reshape(batch, seq, D_MODEL),
            ys.reshape(dec_batch, 1, D_MODEL),
            kv_out(k_t),
            kv_out(v_t),
            hre_p.reshape(1, batch, g, s),
            him_p.reshape(1, batch, g, s),
            conv_p.reshape(1, batch, 2, D_FF),
            k_s.reshape(1, dec_batch, 1, N_HEADS, HEAD_DIM),
            v_s.reshape(1, dec_batch, 1, N_HEADS, HEAD_DIM),
            hre_s.reshape(1, dec_batch, g, s),
            him_s.reshape(1, dec_batch, g, s),
            jnp.stack([cs0, cs1], axis=1).reshape(1, dec_batch, 2, D_FF))
```
